```python
import jax
import jax.numpy as jnp
from jax import lax
import numpy as np

D_MODEL = 1024
BATCH = 4
SEQ = 8192
DEPTH = 4

N_A_LAYERS = DEPTH // 2
N_B_LAYERS = DEPTH - N_A_LAYERS
N_HEADS = 8
HEAD_DIM = D_MODEL // N_HEADS
N_KV_HEADS = 2
HEAD_GROUP = N_HEADS // N_KV_HEADS
KV_DIM = N_KV_HEADS * HEAD_DIM
IDX_HEADS = 4
IDX_DIM = 64
IDX_TOPK_MAX = 256
DSA_Q_BLOCK = 128
MOBA_BLOCK = 256
MOBA_TOPK = 3
MOBA_Q_CHUNK = 16
N_GROUPS = 4
EXPERTS_PER_GROUP = 8
N_EXPERTS = N_GROUPS * EXPERTS_PER_GROUP
TOP_EXPERTS = 2
D_EXPERT = 512
MOE_ROW_BLOCK = 128
PLE_DIM = 256
ALPHA = (2.0 * DEPTH) ** 0.25
BETA = (8.0 * DEPTH) ** -0.25
LN_EPS = 1e-5
A_SPLITS = (D_MODEL, D_MODEL + KV_DIM, D_MODEL + 2 * KV_DIM,
            D_MODEL + 2 * KV_DIM + IDX_HEADS * IDX_DIM,
            D_MODEL + 2 * KV_DIM + IDX_HEADS * IDX_DIM + IDX_DIM)
A_COLS = A_SPLITS[-1] + IDX_HEADS

kernel_name = 'yoco_dsa_moba_hmoe_trunk'


def alibi_slopes(n):
    return 2.0 ** (-8.0 * jnp.arange(1, n + 1, dtype=jnp.float32) / n)


def layer_norm(x, g, b):
    xf = x.astype(jnp.float32)
    mu = jnp.mean(xf, axis=-1, keepdims=True)
    var = jnp.mean(jnp.square(xf - mu), axis=-1, keepdims=True)
    y = (xf - mu) * lax.rsqrt(var + LN_EPS) * g.astype(jnp.float32) + b.astype(jnp.float32)
    return y.astype(x.dtype)


def dsa_attention(h, w_in, w_o, slopes):
    bsz, seq, _ = h.shape
    q, k, v, iq, ik, iw = jnp.split(h @ w_in, A_SPLITS, axis=-1)
    q = q.reshape(bsz, seq, N_KV_HEADS, HEAD_GROUP, HEAD_DIM)
    k = k.reshape(bsz, seq, N_KV_HEADS, HEAD_DIM)
    v = v.reshape(bsz, seq, N_KV_HEADS, HEAD_DIM)
    iq = iq.reshape(bsz, seq, IDX_HEADS, IDX_DIM)
    iw = iw * (IDX_HEADS ** -0.5)
    topk = min(IDX_TOPK_MAX, seq // 4)
    n_blk = seq // DSA_Q_BLOCK
    key_pos = jnp.arange(seq)
    b_ix = jnp.arange(bsz)[:, None, None]
    slope = slopes.reshape(N_KV_HEADS, HEAD_GROUP)[None, None, :, :, None]

    def block(i):
        start = i * DSA_Q_BLOCK
        qb = lax.dynamic_slice_in_dim(q, start, DSA_Q_BLOCK, axis=1)
        iqb = lax.dynamic_slice_in_dim(iq, start, DSA_Q_BLOCK, axis=1)
        iwb = lax.dynamic_slice_in_dim(iw, start, DSA_Q_BLOCK, axis=1)
        qpos = start + jnp.arange(DSA_Q_BLOCK)
        rel = jax.nn.relu(jnp.einsum('bqhd,bsd->bqhs', iqb, ik) * (IDX_DIM ** -0.5))
        score = jnp.einsum('bqhs,bqh->bqs', rel, iwb).astype(jnp.float32)
        score = jnp.where(key_pos[None, None, :] <= qpos[None, :, None], score, -jnp.inf)
        _, idx = lax.top_k(score, topk)
        kg = k[b_ix, idx]
        vg = v[b_ix, idx]
        logits = jnp.einsum('bqgrd,bqkgd->bqgrk', qb, kg).astype(jnp.float32) * (HEAD_DIM ** -0.5)
        dist = (qpos[None, :, None] - idx).astype(jnp.float32)[:, :, None, None, :]
        logits = logits - slope * dist
        valid = (idx <= qpos[None, :, None])[:, :, None, None, :]
        probs = jax.nn.softmax(jnp.where(valid, logits, -jnp.inf), axis=-1).astype(v.dtype)
        out = jnp.einsum('bqgrk,bqkgd->bqgrd', probs, vg)
        return out.reshape(bsz, DSA_Q_BLOCK, D_MODEL)

    out = lax.map(block, jnp.arange(n_blk))
    out = out.transpose(1, 0, 2, 3).reshape(bsz, seq, D_MODEL)
    return out @ w_o


def shared_block_kv(h, w_kv):
    bsz, seq, _ = h.shape
    k, v = jnp.split(h @ w_kv, 2, axis=-1)
    n_blocks = -(-seq // MOBA_BLOCK)
    pad = n_blocks * MOBA_BLOCK - seq

    def to_blocks(t):
        t = jnp.pad(t, ((0, 0), (0, pad), (0, 0)))
        return t.reshape(bsz, n_blocks, MOBA_BLOCK, N_KV_HEADS, HEAD_DIM).transpose(0, 3, 1, 2, 4)

    k_blocks = to_blocks(k)
    v_blocks = to_blocks(v)
    k_means = jnp.mean(k_blocks.astype(jnp.float32), axis=3).astype(h.dtype)
    return k_blocks, v_blocks, k_means


def moba_attention(h, w_q, w_o, k_blocks, v_blocks, k_means, slopes):
    bsz, seq, _ = h.shape
    n_blocks = k_blocks.shape[2]
    n_sel = max(1, min(MOBA_TOPK, n_blocks - 1))
    n_past_cols = n_sel * MOBA_BLOCK
    q = (h @ w_q).reshape(bsz, seq, N_KV_HEADS, HEAD_GROUP, HEAD_DIM)
    blk_ids = jnp.arange(n_blocks)
    in_blk = jnp.arange(MOBA_BLOCK)
    b_ix = jnp.arange(bsz)[:, None, None, None, None]
    g_ix = jnp.arange(N_KV_HEADS)[None, None, :, None, None]
    slope = slopes.reshape(N_KV_HEADS, HEAD_GROUP)[None, None, :, :, None]
    scale = HEAD_DIM ** -0.5

    def chunk(c):
        start = c * MOBA_Q_CHUNK
        qc = lax.dynamic_slice_in_dim(q, start, MOBA_Q_CHUNK, axis=1)
        qpos = start + jnp.arange(MOBA_Q_CHUNK)
        own = start // MOBA_BLOCK
        gate = jnp.einsum('bqgrd,bgnd->bqgrn', qc, k_means).astype(jnp.float32)
        gate = jnp.where(blk_ids < own, gate, -jnp.inf)
        _, sel = lax.top_k(gate, n_sel)
        ks = k_blocks[b_ix, g_ix, sel]
        vs = v_blocks[b_ix, g_ix, sel]
        ko = lax.dynamic_index_in_dim(k_blocks, own, axis=2, keepdims=False)
        vo = lax.dynamic_index_in_dim(v_blocks, own, axis=2, keepdims=False)
        past = jnp.einsum('bqgrd,bqgrjpd->bqgrjp', qc, ks).astype(jnp.float32) * scale
        kpos = sel[..., None] * MOBA_BLOCK + in_blk
        past_dist = (qpos[None, :, None, None, None, None] - kpos).astype(jnp.float32)
        past = past - slope[..., None] * past_dist
        past = jnp.where((sel < own)[..., None], past, -jnp.inf)
        past = past.reshape(bsz, MOBA_Q_CHUNK, N_KV_HEADS, HEAD_GROUP, n_past_cols)
        selfb = jnp.einsum('bqgrd,bgpd->bqgrp', qc, ko).astype(jnp.float32) * scale
        own_dist = (qpos[:, None] - (own * MOBA_BLOCK + in_blk)[None, :]).astype(jnp.float32)
        selfb = selfb - slope * own_dist[None, :, None, None, :]
        selfb = jnp.where((own_dist >= 0)[None, :, None, None, :], selfb, -jnp.inf)
        probs = jax.nn.softmax(jnp.concatenate([past, selfb], axis=-1), axis=-1).astype(v_blocks.dtype)
        p_past = probs[..., :n_past_cols].reshape(bsz, MOBA_Q_CHUNK, N_KV_HEADS, HEAD_GROUP, n_sel, MOBA_BLOCK)
        p_own = probs[..., n_past_cols:]
        out = (jnp.einsum('bqgrjp,bqgrjpd->bqgrd', p_past, vs)
               + jnp.einsum('bqgrp,bgpd->bqgrd', p_own, vo))
        return out.reshape(bsz, MOBA_Q_CHUNK, D_MODEL)

    out = lax.map(chunk, jnp.arange(seq // MOBA_Q_CHUNK))
    out = out.transpose(1, 0, 2, 3).reshape(bsz, seq, D_MODEL)
    return out @ w_o


def hier_moe(h, w_rg, b_rg, w_re, b_re, w_gate, w_up, w_down):
    bsz, seq, _ = h.shape
    n_tok = bsz * seq
    n_assign = n_tok * TOP_EXPERTS
    xt = h.reshape(n_tok, D_MODEL)
    g_logits = (xt @ w_rg + b_rg).astype(jnp.float32)
    g_prob = jax.nn.softmax(g_logits, axis=-1)
    g_sel = jnp.argmax(g_logits, axis=-1)
    g_w = jnp.take_along_axis(g_prob, g_sel[:, None], axis=-1)
    e_logits = (xt @ w_re + b_re).astype(jnp.float32).reshape(n_tok, N_GROUPS, EXPERTS_PER_GROUP)
    e_logits = jnp.take_along_axis(e_logits, g_sel[:, None, None], axis=1)[:, 0]
    e_val, e_loc = lax.top_k(e_logits, TOP_EXPERTS)
    e_w = jax.nn.softmax(e_val, axis=-1) * g_w
    e_id = g_sel[:, None] * EXPERTS_PER_GROUP + e_loc
    flat_e = e_id.reshape(-1)
    flat_tok = jnp.repeat(jnp.arange(n_tok), TOP_EXPERTS)
    flat_w = e_w.reshape(-1)
    order = jnp.argsort(flat_e)
    se, stok, sw = flat_e[order], flat_tok[order], flat_w[order]
    counts = jnp.bincount(flat_e, length=N_EXPERTS)
    padded = (counts + MOE_ROW_BLOCK - 1) // MOE_ROW_BLOCK * MOE_ROW_BLOCK
    pad_end = jnp.cumsum(padded)
    pad_start = pad_end - padded
    raw_start = jnp.cumsum(counts) - counts
    dest = pad_start[se] + (jnp.arange(n_assign) - raw_start[se])
    n_rows = (n_assign + N_EXPERTS * (MOE_ROW_BLOCK - 1) + MOE_ROW_BLOCK - 1) // MOE_ROW_BLOCK * MOE_ROW_BLOCK
    n_row_blocks = n_rows // MOE_ROW_BLOCK
    rows = jnp.zeros((n_rows, D_MODEL), h.dtype).at[dest].set(xt[stok])
    blk_expert = jnp.minimum(
        jnp.searchsorted(pad_end, jnp.arange(n_row_blocks) * MOE_ROW_BLOCK, side='right'),
        N_EXPERTS - 1)

    def expert_block(args):
        xb, e = args
        return (jax.nn.silu(xb @ w_gate[e]) * (xb @ w_up[e])) @ w_down[e]

    yb = lax.map(expert_block, (rows.reshape(n_row_blocks, MOE_ROW_BLOCK, D_MODEL), blk_expert))
    y = yb.reshape(n_rows, D_MODEL)[dest] * sw[:, None].astype(h.dtype)
    out = jax.ops.segment_sum(y, stok, num_segments=n_tok)
    return out.reshape(bsz, seq, D_MODEL)


def setup_inputs(seed: int = 0) -> dict:
    key = jax.random.key(seed)
    ks = jax.random.split(key, 18)
    f32 = jnp.float32

    def nrm(k, shape, scale):
        return jax.random.normal(k, shape, f32) * scale

    a_col_scale = jnp.concatenate([
        jnp.ones((D_MODEL + KV_DIM,), f32),
        jnp.full((KV_DIM,), BETA, f32),
        jnp.ones((A_COLS - D_MODEL - 2 * KV_DIM,), f32)])
    kv_col_scale = jnp.concatenate([jnp.ones((KV_DIM,), f32), jnp.full((KV_DIM,), BETA, f32)])
    return {
        'x': nrm(ks[0], (BATCH, SEQ, D_MODEL), 1.0),
        'p': nrm(ks[1], (DEPTH, BATCH, SEQ, PLE_DIM), 1.0),
        'w_in_a': nrm(ks[2], (N_A_LAYERS, D_MODEL, A_COLS), D_MODEL ** -0.5) * a_col_scale,
        'w_o_a': nrm(ks[3], (N_A_LAYERS, D_MODEL, D_MODEL), BETA * D_MODEL ** -0.5),
        'w_kv_shared': nrm(ks[4], (D_MODEL, 2 * KV_DIM), D_MODEL ** -0.5) * kv_col_scale,
        'w_q_b': nrm(ks[5], (N_B_LAYERS, D_MODEL, D_MODEL), D_MODEL ** -0.5),
        'w_o_b': nrm(ks[6], (N_B_LAYERS, D_MODEL, D_MODEL), BETA * D_MODEL ** -0.5),
        'w_router_group': nrm(ks[7], (DEPTH, D_MODEL, N_GROUPS), D_MODEL ** -0.5),
        'b_router_group': nrm(ks[8], (DEPTH, N_GROUPS), 0.01),
        'w_router_expert': nrm(ks[9], (DEPTH, D_MODEL, N_EXPERTS), D_MODEL ** -0.5),
        'b_router_expert': nrm(ks[10], (DEPTH, N_EXPERTS), 0.01),
        'w_gate': nrm(ks[11], (DEPTH, N_EXPERTS, D_MODEL, D_EXPERT), D_MODEL ** -0.5),
        'w_up': nrm(ks[12], (DEPTH, N_EXPERTS, D_MODEL, D_EXPERT), D_MODEL ** -0.5),
        'w_down': nrm(ks[13], (DEPTH, N_EXPERTS, D_EXPERT, D_MODEL), BETA * D_EXPERT ** -0.5),
        'w_ple': nrm(ks[14], (DEPTH, PLE_DIM, D_MODEL), BETA * PLE_DIM ** -0.5),
        'w_ple_gate': nrm(ks[15], (DEPTH, D_MODEL, D_MODEL), D_MODEL ** -0.5),
        'ln_g': 1.0 + nrm(ks[16], (DEPTH, 3, D_MODEL), 0.02),
        'ln_b': nrm(ks[17], (DEPTH, 3, D_MODEL), 0.02),
    }


def reference(x, p, w_in_a, w_o_a, w_kv_shared, w_q_b, w_o_b, w_router_group, b_router_group,
              w_router_expert, b_router_expert, w_gate, w_up, w_down, w_ple, w_ple_gate, ln_g, ln_b):
    slopes = alibi_slopes(N_HEADS)
    h = x
    for i in range(DEPTH):
        if i < N_A_LAYERS:
            mix = dsa_attention(h, w_in_a[i], w_o_a[i], slopes)
        else:
            if i == N_A_LAYERS:
                k_blocks, v_blocks, k_means = shared_block_kv(h, w_kv_shared)
            j = i - N_A_LAYERS
            mix = moba_attention(h, w_q_b[j], w_o_b[j], k_blocks, v_blocks, k_means, slopes)
        h = layer_norm(ALPHA * h + mix, ln_g[i, 0], ln_b[i, 0])
        ffn = hier_moe(h, w_router_group[i], b_router_group[i], w_router_expert[i],
                       b_router_expert[i], w_gate[i], w_up[i], w_down[i])
        h = layer_norm(ALPHA * h + ffn, ln_g[i, 1], ln_b[i, 1])
        ple = (p[i] @ w_ple[i]) * jax.nn.sigmoid(h @ w_ple_gate[i])
        h = layer_norm(ALPHA * h + ple, ln_g[i, 2], ln_b[i, 2])
    return h
```

```python
import functools

import jax
import jax.numpy as jnp
from jax import lax
from jax.experimental import pallas as pl
from jax.experimental.pallas import tpu as pltpu

D_MODEL = 1024
N_HEADS = 8
HEAD_DIM = 128
N_KV_HEADS = 2
HEAD_GROUP = N_HEADS // N_KV_HEADS
KV_DIM = N_KV_HEADS * HEAD_DIM
IDX_HEADS = 4
IDX_DIM = 64
IDX_TOPK_MAX = 256
MOBA_BLOCK = 256
MOBA_TOPK = 3
N_GROUPS = 4
EXPERTS_PER_GROUP = 8
N_EXPERTS = N_GROUPS * EXPERTS_PER_GROUP
D_EXPERT = 512
LN_EPS = 1e-5

LANES = 128
DSA_Q_TILE = 128
DSA_K_TILE = 512
ROW_TILE = 256
PROJ_TILE = 512
VMEM_LIMIT_BYTES = 52 * 1024 * 1024

ROUTER_COL0 = N_GROUPS
INT_MIN = -2 ** 31
NEG_INF_KEY = -2139095041
M_INIT = -1e30

_f32 = jnp.float32
_bf16 = jnp.bfloat16


def _params(n_axes):
    return pltpu.CompilerParams(dimension_semantics=("arbitrary",) * n_axes,
                                vmem_limit_bytes=VMEM_LIMIT_BYTES)


def _layer_norm(z, g, b):
    mu = jnp.mean(z, axis=-1, keepdims=True)
    zc = z - mu
    var = jnp.mean(zc * zc, axis=-1, keepdims=True)
    return zc * lax.rsqrt(var + LN_EPS) * g + b


def _head_slopes():
    return tuple(2.0 ** (-8.0 * (h + 1) / N_HEADS) for h in range(N_HEADS))


def _proj_kernel(x_ref, *refs, scales):
    n = len(scales)
    x = x_ref[...].astype(_bf16)
    for w_ref, o_ref, s in zip(refs[:n], refs[n:], scales):
        y = jnp.dot(x, w_ref[...], preferred_element_type=_f32)
        if s != 1.0:
            y = y * s
        o_ref[...] = y.astype(o_ref.dtype)


def _project(x, weights, out_dtypes, scales):
    n_tok, k = x.shape
    tm = min(PROJ_TILE, n_tok)
    in_specs = [pl.BlockSpec((tm, k), lambda i: (i, 0))]
    in_specs += [pl.BlockSpec(w.shape, lambda i: (0, 0)) for w in weights]
    return pl.pallas_call(
        functools.partial(_proj_kernel, scales=tuple(scales)),
        grid=(n_tok // tm,),
        in_specs=in_specs,
        out_specs=[pl.BlockSpec((tm, w.shape[1]), lambda i: (i, 0)) for w in weights],
        out_shape=[jax.ShapeDtypeStruct((n_tok, w.shape[1]), dt) for w, dt in zip(weights, out_dtypes)],
        compiler_params=_params(1),
        name="proj",
    )(x, *weights)


def _kv_kernel(x_ref, wk_ref, wv_ref, k_ref, v_ref, km_ref):
    x = x_ref[...].astype(_bf16)
    k = jnp.dot(x, wk_ref[...], preferred_element_type=_f32)
    v = jnp.dot(x, wv_ref[...], preferred_element_type=_f32)
    k_ref[...] = k.astype(_bf16)
    v_ref[...] = v.astype(_bf16)
    km_ref[0] = jnp.mean(k, axis=0, keepdims=True)


def _shared_kv(x, wk, wv):
    n_tok, k = x.shape
    tm = MOBA_BLOCK
    return pl.pallas_call(
        _kv_kernel,
        grid=(n_tok // tm,),
        in_specs=[pl.BlockSpec((tm, k), lambda i: (i, 0)),
                  pl.BlockSpec(wk.shape, lambda i: (0, 0)),
                  pl.BlockSpec(wv.shape, lambda i: (0, 0))],
        out_specs=[pl.BlockSpec((tm, KV_DIM), lambda i: (i, 0)),
                   pl.BlockSpec((tm, KV_DIM), lambda i: (i, 0)),
                   pl.BlockSpec((1, 1, KV_DIM), lambda i: (i, 0, 0))],
        out_shape=[jax.ShapeDtypeStruct((n_tok, KV_DIM), _bf16),
                   jax.ShapeDtypeStruct((n_tok, KV_DIM), _bf16),
                   jax.ShapeDtypeStruct((n_tok // tm, 1, KV_DIM), _f32)],
        compiler_params=_params(1),
        name="shared_kv",
    )(x, wk, wv)


def _dsa_kernel(q_ref, iq_ref, iw_ref, ikt_ref, kt_ref, v_ref, o_ref,
                keys_ref, qs_ref, m_ref, l_ref, acc_ref, *, topk, seq, slopes):
    tq, tk = DSA_Q_TILE, DSA_K_TILE
    rows = HEAD_GROUP * tq
    q0 = pl.program_id(1) * tq
    n_t = (q0 + tq + tk - 1) // tk
    qpos = q0 + lax.broadcasted_iota(jnp.int32, (tq, 1), 0)
    lane = lax.broadcasted_iota(jnp.int32, (1, tk), 1)

    iq = iq_ref[...]
    iq_heads = [iq[:, h * IDX_DIM:(h + 1) * IDX_DIM].astype(_bf16) for h in range(IDX_HEADS)]
    iw = iw_ref[:, IDX_DIM:IDX_DIM + IDX_HEADS] * (IDX_HEADS ** -0.5)

    def score_body(t, carry):
        ikt = ikt_ref[t].astype(_bf16)
        sc = jnp.zeros((tq, tk), _f32)
        for h in range(IDX_HEADS):
            s = jnp.dot(iq_heads[h], ikt, preferred_element_type=_f32)
            sc = sc + jnp.maximum(s * (IDX_DIM ** -0.5), 0.0) * iw[:, h:h + 1]
        kpos = t * tk + lane
        sc = jnp.where(kpos <= qpos, sc, -jnp.inf)
        bits = pltpu.bitcast(sc, jnp.int32)
        keys = bits ^ ((bits >> 31) & 0x7FFFFFFF)
        keys_ref[t] = jnp.where(sc == 0.0, 0, keys)
        return carry

    lax.fori_loop(0, n_t, score_body, 0)

    def lane_fold(m):
        part = m[:, 0:LANES]
        for c in range(1, tk // LANES):
            part = part + m[:, c * LANES:(c + 1) * LANES]
        return part

    def count_ge(cand):
        def body(t, acc):
            return acc + lane_fold(jnp.where(keys_ref[t] >= cand, 1.0, 0.0))
        acc = lax.fori_loop(0, n_t, body, jnp.zeros((tq, LANES), _f32))
        return jnp.sum(acc, axis=1, keepdims=True)

    def bit_body(i, tu):
        cand_u = tu | lax.shift_left(jnp.int32(1), 31 - i)
        cnt = count_ge(cand_u ^ INT_MIN)
        return jnp.where(cnt >= topk, cand_u, tu)

    tu = lax.fori_loop(0, 32, bit_body, jnp.zeros((tq, 1), jnp.int32))
    thr = tu ^ INT_MIN
    c_ge = count_ge(thr)
    need = topk - count_ge(thr + 1)

    def tie_search():
        n_bits = (seq - 1).bit_length()

        def count_eq_before(pc):
            def body(t, acc):
                kpos = t * tk + lane
                hit = jnp.where(keys_ref[t] == thr, jnp.where(kpos < pc, 1.0, 0.0), 0.0)
                return acc + lane_fold(hit)
            acc = lax.fori_loop(0, n_t, body, jnp.zeros((tq, LANES), _f32))
            return jnp.sum(acc, axis=1, keepdims=True)

        def body(i, p):
            cand = p | lax.shift_left(jnp.int32(1), n_bits - 1 - i)
            return jnp.where(count_eq_before(cand) < need, cand, p)

        return lax.fori_loop(0, n_bits, body, jnp.zeros((tq, 1), jnp.int32))

    excess = jnp.where(c_ge > topk, jnp.where(thr > NEG_INF_KEY, 1.0, 0.0), 0.0)
    pcut = lax.cond(jnp.max(excess) > 0.0, tie_search,
                    lambda: jnp.full((tq, 1), seq, jnp.int32))

    for g in range(N_KV_HEADS):
        qs_ref[g] = jnp.concatenate(
            [q_ref[:, (g * HEAD_GROUP + r) * HEAD_DIM:(g * HEAD_GROUP + r + 1) * HEAD_DIM]
             for r in range(HEAD_GROUP)], axis=0)
    m_ref[...] = jnp.full(m_ref.shape, M_INIT, _f32)
    l_ref[...] = jnp.zeros(l_ref.shape, _f32)
    acc_ref[...] = jnp.zeros(acc_ref.shape, _f32)

    def att_body(t, carry):
        keys = keys_ref[t]
        kpos = t * tk + lane
        take = jnp.where(keys > thr, 0.0,
                         jnp.where(keys == thr, jnp.where(kpos <= pcut, 0.0, -jnp.inf), -jnp.inf))
        take = jnp.where(kpos <= qpos, take, -jnp.inf)
        dist = (kpos - qpos).astype(_f32)
        for g in range(N_KV_HEADS):
            bias = jnp.concatenate(
                [take + slopes[g * HEAD_GROUP + r] * dist for r in range(HEAD_GROUP)], axis=0)
            s = jnp.dot(qs_ref[g], kt_ref[t, g * HEAD_DIM:(g + 1) * HEAD_DIM, :],
                        preferred_element_type=_f32)
            x = s + bias
            m_prev = m_ref[g]
            m_new = jnp.maximum(m_prev, jnp.max(x, axis=1, keepdims=True))
            alpha = jnp.exp(m_prev - m_new)
            p = jnp.exp(x - m_new)
            l_ref[g] = alpha * l_ref[g] + jnp.sum(p, axis=1, keepdims=True)
            acc_ref[g] = alpha * acc_ref[g] + jnp.dot(
                p.astype(_bf16), v_ref[t, :, g * HEAD_DIM:(g + 1) * HEAD_DIM],
                preferred_element_type=_f32)
            m_ref[g] = m_new
        return carry

    lax.fori_loop(0, n_t, att_body, 0)

    for g in range(N_KV_HEADS):
        out = acc_ref[g] / l_ref[g]
        for r in range(HEAD_GROUP):
            h = g * HEAD_GROUP + r
            o_ref[:, h * HEAD_DIM:(h + 1) * HEAD_DIM] = out[r * tq:(r + 1) * tq, :].astype(o_ref.dtype)


def _dsa_attention(q, iq, ikw, k, v, bsz, seq):
    tq, tk = DSA_Q_TILE, DSA_K_TILE
    n_t = seq // tk
    topk = min(IDX_TOPK_MAX, seq // 4)
    q3 = q.reshape(bsz, seq, D_MODEL)
    iq3 = iq.reshape(bsz, seq, IDX_HEADS * IDX_DIM)
    ikw3 = ikw.reshape(bsz, seq, LANES)
    ikt = ikw3[:, :, :IDX_DIM].reshape(bsz, n_t, tk, IDX_DIM).transpose(0, 1, 3, 2)
    kt = k.reshape(bsz, n_t, tk, KV_DIM).transpose(0, 1, 3, 2)
    v4 = v.reshape(bsz, n_t, tk, KV_DIM)
    rows = HEAD_GROUP * tq
    out = pl.pallas_call(
        functools.partial(_dsa_kernel, topk=topk, seq=seq, slopes=_head_slopes()),
        grid=(bsz, seq // tq),
        in_specs=[pl.BlockSpec((None, tq, D_MODEL), lambda b, i: (b, i, 0)),
                  pl.BlockSpec((None, tq, IDX_HEADS * IDX_DIM), lambda b, i: (b, i, 0)),
                  pl.BlockSpec((None, tq, LANES), lambda b, i: (b, i, 0)),
                  pl.BlockSpec((None, n_t, IDX_DIM, tk), lambda b, i: (b, 0, 0, 0)),
                  pl.BlockSpec((None, n_t, KV_DIM, tk), lambda b, i: (b, 0, 0, 0)),
                  pl.BlockSpec((None, n_t, tk, KV_DIM), lambda b, i: (b, 0, 0, 0))],
        out_specs=pl.BlockSpec((None, tq, D_MODEL), lambda b, i: (b, i, 0)),
        out_shape=jax.ShapeDtypeStruct((bsz, seq, D_MODEL), _bf16),
        scratch_shapes=[pltpu.VMEM((n_t, tq, tk), jnp.int32),
                        pltpu.VMEM((N_KV_HEADS, rows, HEAD_DIM), _bf16),
                        pltpu.VMEM((N_KV_HEADS, rows, 1), _f32),
                        pltpu.VMEM((N_KV_HEADS, rows, 1), _f32),
                        pltpu.VMEM((N_KV_HEADS, rows, HEAD_DIM), _f32)],
        compiler_params=_params(2),
        name="dsa_attention",
    )(q3, iq3, ikw3, ikt, kt, v4)
    return out.reshape(bsz * seq, D_MODEL)


def _moba_kernel(q_ref, kmt_ref, kt_ref, v_ref, o_ref,
                 qs_ref, sel_ref, tile_ref, m_ref, l_ref, acc_ref, *, n_sel, slopes):
    blk = MOBA_BLOCK
    rows = HEAD_GROUP * blk
    own = pl.program_id(1)
    colf = lax.broadcasted_iota(jnp.int32, (1, LANES), 1).astype(_f32)
    row = lax.broadcasted_iota(jnp.int32, (rows, 1), 0)
    rowq = row % blk
    kcol = lax.broadcasted_iota(jnp.int32, (1, blk), 1)
    ownf = own.astype(_f32)

    for g in range(N_KV_HEADS):
        qs_ref[g] = jnp.concatenate(
            [q_ref[:, (g * HEAD_GROUP + r) * HEAD_DIM:(g * HEAD_GROUP + r + 1) * HEAD_DIM]
             for r in range(HEAD_GROUP)], axis=0)

    slope_cols = []
    for g in range(N_KV_HEADS):
        sc = jnp.full((rows, 1), slopes[g * HEAD_GROUP], _f32)
        for r in range(1, HEAD_GROUP):
            sc = jnp.where(row >= r * blk, slopes[g * HEAD_GROUP + r], sc)
        slope_cols.append(sc)

    for g in range(N_KV_HEADS):
        qg = qs_ref[g]
        dsl = slice(g * HEAD_DIM, (g + 1) * HEAD_DIM)
        gate = (jnp.dot(qg, kmt_ref[0, dsl, :], preferred_element_type=_f32)
                + jnp.dot(qg, kmt_ref[1, dsl, :], preferred_element_type=_f32))
        gl = jnp.where(colf < ownf, gate, -jnp.inf)
        sel = jnp.zeros((rows, LANES), _f32)
        for _ in range(n_sel):
            mx = jnp.max(gl, axis=1, keepdims=True)
            cand = jnp.where(gl == mx, colf, 1e9)
            cand = jnp.where(mx > -jnp.inf, cand, 1e9)
            pick = colf == jnp.min(cand, axis=1, keepdims=True)
            sel = jnp.where(pick, 1.0, sel)
            gl = jnp.where(pick, -jnp.inf, gl)
        sel_ref[g] = sel

        tile = slope_cols[g] * (kcol - rowq).astype(_f32)
        tile_ref[g] = tile
        s = jnp.dot(qg, kt_ref[own, dsl, :], preferred_element_type=_f32)
        x = s + jnp.where(kcol <= rowq, tile, -jnp.inf)
        m = jnp.max(x, axis=1, keepdims=True)
        p = jnp.exp(x - m)
        m_ref[g] = m
        l_ref[g] = jnp.sum(p, axis=1, keepdims=True)
        acc_ref[g] = jnp.dot(p.astype(_bf16), v_ref[own, :, dsl], preferred_element_type=_f32)

    def blk_body(n, carry):
        nf = n.astype(_f32)
        for g in range(N_KV_HEADS):
            dsl = slice(g * HEAD_DIM, (g + 1) * HEAD_DIM)
            chosen = jnp.sum(jnp.where(colf == nf, sel_ref[g], 0.0), axis=1, keepdims=True)

            @pl.when(jnp.max(chosen) > 0.0)
            def _():
                off = ((n - own) * blk).astype(_f32)
                rowc = jnp.where(chosen > 0.0, slope_cols[g] * off, -jnp.inf)
                s = jnp.dot(qs_ref[g], kt_ref[n, dsl, :], preferred_element_type=_f32)
                x = s + tile_ref[g] + rowc
                m_prev = m_ref[g]
                m_new = jnp.maximum(m_prev, jnp.max(x, axis=1, keepdims=True))
                alpha = jnp.exp(m_prev - m_new)
                p = jnp.exp(x - m_new)
                l_ref[g] = alpha * l_ref[g] + jnp.sum(p, axis=1, keepdims=True)
                acc_ref[g] = alpha * acc_ref[g] + jnp.dot(
                    p.astype(_bf16), v_ref[n, :, dsl], preferred_element_type=_f32)
                m_ref[g] = m_new
        return carry

    lax.fori_loop(0, own, blk_body, 0)

    for g in range(N_KV_HEADS):
        out = acc_ref[g] / l_ref[g]
        for r in range(HEAD_GROUP):
            h = g * HEAD_GROUP + r
            o_ref[:, h * HEAD_DIM:(h + 1) * HEAD_DIM] = out[r * blk:(r + 1) * blk, :].astype(o_ref.dtype)


def _moba_attention(q, kt, v4, kmt, bsz, seq):
    blk = MOBA_BLOCK
    n_blocks = seq // blk
    n_sel = max(1, min(MOBA_TOPK, n_blocks - 1))
    rows = HEAD_GROUP * blk
    out = pl.pallas_call(
        functools.partial(_moba_kernel, n_sel=n_sel, slopes=_head_slopes()),
        grid=(bsz, n_blocks),
        in_specs=[pl.BlockSpec((None, blk, D_MODEL), lambda b, i: (b, i, 0)),
                  pl.BlockSpec((None, 2, KV_DIM, LANES), lambda b, i: (b, 0, 0, 0)),
                  pl.BlockSpec((None, n_blocks, KV_DIM, blk), lambda b, i: (b, 0, 0, 0)),
                  pl.BlockSpec((None, n_blocks, blk, KV_DIM), lambda b, i: (b, 0, 0, 0))],
        out_specs=pl.BlockSpec((None, blk, D_MODEL), lambda b, i: (b, i, 0)),
        out_shape=jax.ShapeDtypeStruct((bsz, seq, D_MODEL), _bf16),
        scratch_shapes=[pltpu.VMEM((N_KV_HEADS, rows, HEAD_DIM), _bf16),
                        pltpu.VMEM((N_KV_HEADS, rows, LANES), _f32),
                        pltpu.VMEM((N_KV_HEADS, rows, blk), _f32),
                        pltpu.VMEM((N_KV_HEADS, rows, 1), _f32),
                        pltpu.VMEM((N_KV_HEADS, rows, 1), _f32),
                        pltpu.VMEM((N_KV_HEADS, rows, HEAD_DIM), _f32)],
        compiler_params=_params(2),
        name="moba_attention",
    )(q.reshape(bsz, seq, D_MODEL), kmt, kt, v4)
    return out.reshape(bsz * seq, D_MODEL)


def _mix_router_kernel(a_ref, wo_ref, h_ref, g_ref, b_ref, wr_ref, br_ref,
                       h1_ref, info_ref, cnt_ref, carry_ref, *, alpha):
    tm = a_ref.shape[0]

    @pl.when(pl.program_id(0) == 0)
    def _():
        carry_ref[...] = jnp.zeros(carry_ref.shape, _f32)

    mix = jnp.dot(a_ref[...], wo_ref[...], preferred_element_type=_f32)
    h1 = _layer_norm(alpha * h_ref[...] + mix, g_ref[...], b_ref[...])
    h1_ref[...] = h1

    logits = jnp.dot(h1, wr_ref[...], preferred_element_type=_f32,
                     precision=lax.Precision.HIGHEST) + br_ref[...]
    col = lax.broadcasted_iota(jnp.int32, (1, LANES), 1).astype(_f32)
    big = 1e9
    is_group = col < float(N_GROUPS)
    gl = jnp.where(is_group, logits, -jnp.inf)
    gmax = jnp.max(gl, axis=1, keepdims=True)
    gsel = jnp.min(jnp.where(gl == gmax, col, big), axis=1, keepdims=True)
    denom = jnp.sum(jnp.where(is_group, jnp.exp(logits - gmax), 0.0), axis=1, keepdims=True)
    g_w = 1.0 / denom

    lo = ROUTER_COL0 + EXPERTS_PER_GROUP * gsel
    in_group = jnp.where(col >= lo, jnp.where(col < lo + EXPERTS_PER_GROUP, 1.0, 0.0), 0.0)
    el = jnp.where(in_group > 0.0, logits, -jnp.inf)
    v1 = jnp.max(el, axis=1, keepdims=True)
    i1 = jnp.min(jnp.where(el == v1, col, big), axis=1, keepdims=True)
    el2 = jnp.where(col == i1, -jnp.inf, el)
    v2 = jnp.max(el2, axis=1, keepdims=True)
    i2 = jnp.min(jnp.where(el2 == v2, col, big), axis=1, keepdims=True)
    t = jnp.exp(v2 - v1)
    w1 = g_w * (1.0 / (1.0 + t))
    w2 = g_w * (t / (1.0 + t))

    o1 = jnp.where(col == i1, 1.0, 0.0)
    o2 = jnp.where(col == i2, 1.0, 0.0)
    both = o1 + o2
    ri = lax.broadcasted_iota(jnp.int32, (tm, tm), 0)
    ci = lax.broadcasted_iota(jnp.int32, (tm, tm), 1)
    earlier = jnp.where(ci < ri, 1.0, 0.0).astype(_bf16)
    before = jnp.dot(earlier, both.astype(_bf16), preferred_element_type=_f32) + carry_ref[...]
    r1 = jnp.sum(o1 * before, axis=1, keepdims=True)
    r2 = jnp.sum(o2 * before, axis=1, keepdims=True)
    carry_ref[...] = carry_ref[...] + jnp.sum(both, axis=0, keepdims=True)
    cnt_ref[...] = carry_ref[...]

    info = jnp.where(col == 0.0, i1 - ROUTER_COL0, 0.0)
    info = jnp.where(col == 1.0, i2 - ROUTER_COL0, info)
    info = jnp.where(col == 2.0, w1, info)
    info = jnp.where(col == 3.0, w2, info)
    info = jnp.where(col == 4.0, r1, info)
    info = jnp.where(col == 5.0, r2, info)
    info_ref[...] = info


def _mix_router(a, wo, h, ln_g, ln_b, wr, br, alpha):
    n_tok = h.shape[0]
    tm = min(ROW_TILE, n_tok)
    row = lambda i: (i, 0)
    fixed = lambda i: (0, 0)
    return pl.pallas_call(
        functools.partial(_mix_router_kernel, alpha=alpha),
        grid=(n_tok // tm,),
        in_specs=[pl.BlockSpec((tm, D_MODEL), row),
                  pl.BlockSpec((D_MODEL, D_MODEL), fixed),
                  pl.BlockSpec((tm, D_MODEL), row),
                  pl.BlockSpec((1, D_MODEL), fixed),
                  pl.BlockSpec((1, D_MODEL), fixed),
                  pl.BlockSpec((D_MODEL, LANES), fixed),
                  pl.BlockSpec((1, LANES), fixed)],
        out_specs=[pl.BlockSpec((tm, D_MODEL), row),
                   pl.BlockSpec((tm, LANES), row),
                   pl.BlockSpec((1, LANES), fixed)],
        out_shape=[jax.ShapeDtypeStruct((n_tok, D_MODEL), _f32),
                   jax.ShapeDtypeStruct((n_tok, LANES), _f32),
                   jax.ShapeDtypeStruct((1, LANES), _f32)],
        scratch_shapes=[pltpu.VMEM((1, LANES), _f32)],
        compiler_params=_params(1),
        name="mix_ln_router",
    )(a, wo, h, ln_g, ln_b, wr, br)


def _row_copy(src_ref, src_row, dst_ref, dst_row, sem):
    return pltpu.make_async_copy(src_ref.at[pl.ds(src_row, 1)], dst_ref.at[pl.ds(dst_row, 1)], sem)


def _scatter_kernel(dest_ref, x_ref, zero_ref, out_ref, sem):
    del zero_ref
    ts = x_ref.shape[0]

    def start(j, carry):
        for k in range(2):
            _row_copy(x_ref, j, out_ref, dest_ref[0, 0, 2 * j + k], sem).start()
        return carry

    lax.fori_loop(0, ts, start, 0)

    def wait(j, carry):
        _row_copy(x_ref, 0, out_ref, 0, sem).wait()
        return carry

    lax.fori_loop(0, 2 * ts, wait, 0)


def _moe_scatter(h1, dest, n_rows):
    n_tok = h1.shape[0]
    ts = min(ROW_TILE, n_tok)
    zeros = jnp.zeros((n_rows, D_MODEL), _f32)
    return pl.pallas_call(
        _scatter_kernel,
        grid=(n_tok // ts,),
        in_specs=[pl.BlockSpec((1, 1, 2 * ts), lambda i: (i, 0, 0), memory_space=pltpu.SMEM),
                  pl.BlockSpec((ts, D_MODEL), lambda i: (i, 0)),
                  pl.BlockSpec(memory_space=pl.ANY)],
        out_specs=pl.BlockSpec(memory_space=pl.ANY),
        out_shape=jax.ShapeDtypeStruct((n_rows, D_MODEL), _f32),
        scratch_shapes=[pltpu.SemaphoreType.DMA(())],
        input_output_aliases={2: 0},
        compiler_params=_params(1),
        name="moe_scatter",
    )(dest.reshape(n_tok // ts, 1, 2 * ts), h1, zeros)


def _ffn_kernel(be_ref, nu_ref, x_ref, wg_ref, wu_ref, wd_ref, y_ref):
    del be_ref
    i = pl.program_id(0)

    @pl.when(i < nu_ref[0])
    def _():
        x = x_ref[...].astype(_bf16)
        gate = jnp.dot(x, wg_ref[...], preferred_element_type=_f32)
        up = jnp.dot(x, wu_ref[...], preferred_element_type=_f32)
        act = gate * (1.0 / (1.0 + jnp.exp(-gate))) * up
        y_ref[...] = jnp.dot(act.astype(_bf16), wd_ref[...], preferred_element_type=_f32)

    @pl.when(i >= nu_ref[0])
    def _():
        y_ref[...] = jnp.zeros(y_ref.shape, _f32)


def _moe_ffn(rows, blk_expert, n_used, wg, wu, wd):
    n_rows = rows.shape[0]
    tb = ROW_TILE
    grid_spec = pltpu.PrefetchScalarGridSpec(
        num_scalar_prefetch=2,
        grid=(n_rows // tb,),
        in_specs=[pl.BlockSpec((tb, D_MODEL), lambda i, be, nu: (i, 0)),
                  pl.BlockSpec((None, D_MODEL, D_EXPERT), lambda i, be, nu: (be[i], 0, 0)),
                  pl.BlockSpec((None, D_MODEL, D_EXPERT), lambda i, be, nu: (be[i], 0, 0)),
                  pl.BlockSpec((None, D_EXPERT, D_MODEL), lambda i, be, nu: (be[i], 0, 0))],
        out_specs=pl.BlockSpec((tb, D_MODEL), lambda i, be, nu: (i, 0)),
    )
    return pl.pallas_call(
        _ffn_kernel,
        grid_spec=grid_spec,
        out_shape=jax.ShapeDtypeStruct((n_rows, D_MODEL), _f32),
        compiler_params=_params(1),
        name="moe_ffn",
    )(blk_expert, n_used, rows, wg, wu, wd)


def _combine_kernel(dest_ref, h_ref, info_ref, y_ref, g_ref, b_ref, o_ref, buf_ref, sem, *, alpha):
    ts = h_ref.shape[0]

    def start(j, carry):
        for k in range(2):
            _row_copy(y_ref, dest_ref[0, 0, 2 * j + k], buf_ref.at[k], j, sem).start()
        return carry

    lax.fori_loop(0, ts, start, 0)

    def wait(j, carry):
        _row_copy(y_ref, 0, buf_ref.at[0], 0, sem).wait()
        return carry

    lax.fori_loop(0, 2 * ts, wait, 0)

    ffn = buf_ref[0] * info_ref[:, 2:3] + buf_ref[1] * info_ref[:, 3:4]
    o_ref[...] = _layer_norm(alpha * h_ref[...] + ffn, g_ref[...], b_ref[...])


def _moe_combine(h1, info, y, dest, ln_g, ln_b, alpha):
    n_tok = h1.shape[0]
    ts = min(ROW_TILE, n_tok)
    row = lambda i: (i, 0)
    fixed = lambda i: (0, 0)
    return pl.pallas_call(
        functools.partial(_combine_kernel, alpha=alpha),
        grid=(n_tok // ts,),
        in_specs=[pl.BlockSpec((1, 1, 2 * ts), lambda i: (i, 0, 0), memory_space=pltpu.SMEM),
                  pl.BlockSpec((ts, D_MODEL), row),
                  pl.BlockSpec((ts, LANES), row),
                  pl.BlockSpec(memory_space=pl.ANY),
                  pl.BlockSpec((1, D_MODEL), fixed),
                  pl.BlockSpec((1, D_MODEL), fixed)],
        out_specs=pl.BlockSpec((ts, D_MODEL), row),
        out_shape=jax.ShapeDtypeStruct((n_tok, D_MODEL), _f32),
        scratch_shapes=[pltpu.VMEM((2, ts, D_MODEL), _f32), pltpu.SemaphoreType.DMA(())],
        compiler_params=_params(1),
        name="moe_combine_ln",
    )(dest.reshape(n_tok // ts, 1, 2 * ts), h1, info, y, ln_g, ln_b)


def _hier_moe(h1, info, counts, wg, wu, wd, ln_g, ln_b, alpha):
    n_tok = h1.shape[0]
    tb = ROW_TILE
    cnt = counts[0, ROUTER_COL0:ROUTER_COL0 + N_EXPERTS].astype(jnp.int32)
    padded = (cnt + tb - 1) // tb * tb
    pad_end = jnp.cumsum(padded)
    pad_start = pad_end - padded
    n_rows = (2 * n_tok + N_EXPERTS * (tb - 1) + tb - 1) // tb * tb
    expert = info[:, 0:2].astype(jnp.int32)
    rank = info[:, 4:6].astype(jnp.int32)
    dest = pad_start[expert] + rank
    blk_expert = jnp.minimum(
        jnp.searchsorted(pad_end, jnp.arange(n_rows // tb, dtype=jnp.int32) * tb, side='right'),
        N_EXPERTS - 1).astype(jnp.int32)
    n_used = (pad_end[-1:] // tb).astype(jnp.int32)
    rows = _moe_scatter(h1, dest, n_rows)
    y = _moe_ffn(rows, blk_expert, n_used, wg, wu, wd)
    return _moe_combine(h1, info, y, dest, ln_g, ln_b, alpha)


def _ple_kernel(h_ref, p_ref, wp_ref, wg_ref, g_ref, b_ref, o_ref, *, alpha):
    h = h_ref[...]
    emb = jnp.dot(p_ref[...].astype(_bf16), wp_ref[...], preferred_element_type=_f32)
    gate = jnp.dot(h.astype(_bf16), wg_ref[...], preferred_element_type=_f32)
    ple = emb * (1.0 / (1.0 + jnp.exp(-gate)))
    o_ref[...] = _layer_norm(alpha * h + ple, g_ref[...], b_ref[...])


def _ple(h, p, wp, wg, ln_g, ln_b, alpha):
    n_tok = h.shape[0]
    tm = min(ROW_TILE, n_tok)
    row = lambda i: (i, 0)
    fixed = lambda i: (0, 0)
    return pl.pallas_call(
        functools.partial(_ple_kernel, alpha=alpha),
        grid=(n_tok // tm,),
        in_specs=[pl.BlockSpec((tm, D_MODEL), row),
                  pl.BlockSpec((tm, p.shape[1]), row),
                  pl.BlockSpec(wp.shape, fixed),
                  pl.BlockSpec(wg.shape, fixed),
                  pl.BlockSpec((1, D_MODEL), fixed),
                  pl.BlockSpec((1, D_MODEL), fixed)],
        out_specs=pl.BlockSpec((tm, D_MODEL), row),
        out_shape=jax.ShapeDtypeStruct((n_tok, D_MODEL), _f32),
        compiler_params=_params(1),
        name="ple_ln",
    )(h, p, wp, wg, ln_g, ln_b)


def kernel(x, p, w_in_a, w_o_a, w_kv_shared, w_q_b, w_o_b, w_router_group, b_router_group,
           w_router_expert, b_router_expert, w_gate, w_up, w_down, w_ple, w_ple_gate, ln_g, ln_b):
    bsz, seq, _ = x.shape
    depth = w_gate.shape[0]
    n_a = w_in_a.shape[0]
    alpha = (2.0 * depth) ** 0.25
    n_tok = bsz * seq
    assert seq % DSA_K_TILE == 0 and seq % MOBA_BLOCK == 0 and n_tok % ROW_TILE == 0
    scale = HEAD_DIM ** -0.5
    c_k = D_MODEL + KV_DIM
    c_v = D_MODEL + 2 * KV_DIM
    c_iq = c_v + IDX_HEADS * IDX_DIM

    h = x.reshape(n_tok, D_MODEL)
    kt = v4 = kmt = None
    for i in range(depth):
        if i < n_a:
            w = w_in_a[i]
            w_ikw = jnp.pad(w[:, c_iq:], ((0, 0), (0, LANES - (w.shape[1] - c_iq))))
            q, k, v, iq, ikw = _project(
                h,
                [w[:, :D_MODEL].astype(_bf16), w[:, D_MODEL:c_k].astype(_bf16),
                 w[:, c_k:c_v].astype(_bf16), w[:, c_v:c_iq].astype(_bf16), w_ikw.astype(_bf16)],
                [_bf16, _bf16, _bf16, _f32, _f32],
                [scale, 1.0, 1.0, 1.0, 1.0])
            attn = _dsa_attention(q, iq, ikw, k, v, bsz, seq)
            w_o = w_o_a[i]
        else:
            j = i - n_a
            if kt is None:
                n_blocks = seq // MOBA_BLOCK
                k, v, km = _shared_kv(h, w_kv_shared[:, :KV_DIM].astype(_bf16),
                                      w_kv_shared[:, KV_DIM:].astype(_bf16))
                kt = k.reshape(bsz, n_blocks, MOBA_BLOCK, KV_DIM).transpose(0, 1, 3, 2)
                v4 = v.reshape(bsz, n_blocks, MOBA_BLOCK, KV_DIM)
                kmt = km.reshape(bsz, n_blocks, KV_DIM).transpose(0, 2, 1)
                kmt = jnp.pad(kmt, ((0, 0), (0, 0), (0, LANES - n_blocks)))
                km_hi = kmt.astype(_bf16)
                km_lo = (kmt - km_hi.astype(_f32)).astype(_bf16)
                kmt = jnp.stack([km_hi, km_lo], axis=1)
            (q,) = _project(h, [w_q_b[j].astype(_bf16)], [_bf16], [scale])
            attn = _moba_attention(q, kt, v4, kmt, bsz, seq)
            w_o = w_o_b[j]

        w_r = jnp.concatenate([w_router_group[i], w_router_expert[i]], axis=1)
        w_r = jnp.pad(w_r, ((0, 0), (0, LANES - w_r.shape[1])))
        b_r = jnp.concatenate([b_router_group[i], b_router_expert[i]])
        b_r = jnp.pad(b_r, (0, LANES - b_r.shape[0])).reshape(1, LANES)
        h1, info, counts = _mix_router(attn, w_o.astype(_bf16), h, ln_g[i, 0].reshape(1, -1),
                                       ln_b[i, 0].reshape(1, -1), w_r, b_r, alpha)
        h2 = _hier_moe(h1, info, counts, w_gate[i].astype(_bf16), w_up[i].astype(_bf16),
                       w_down[i].astype(_bf16), ln_g[i, 1].reshape(1, -1), ln_b[i, 1].reshape(1, -1),
                       alpha)
        h = _ple(h2, p[i].reshape(n_tok, -1), w_ple[i].astype(_bf16), w_ple_gate[i].astype(_bf16),
                 ln_g[i, 2].reshape(1, -1), ln_b[i, 2].reshape(1, -1), alpha)
    return h.reshape(bsz, seq, D_MODEL)
```

```python
import functools

import jax
import jax.numpy as jnp
from jax import lax
from jax.experimental import pallas as pl
from jax.experimental.pallas import tpu as pltpu

D_MODEL = 1024
N_HEADS = 8
HEAD_DIM = 128
N_KV_HEADS = 2
HEAD_GROUP = N_HEADS // N_KV_HEADS
KV_DIM = N_KV_HEADS * HEAD_DIM
IDX_HEADS = 4
IDX_DIM = 64
IDX_TOPK_MAX = 256
MOBA_BLOCK = 256
MOBA_TOPK = 3
N_GROUPS = 4
EXPERTS_PER_GROUP = 8
N_EXPERTS = N_GROUPS * EXPERTS_PER_GROUP
D_EXPERT = 512
LN_EPS = 1e-5

LANES = 128
DSA_Q_TILE = 128
DSA_K_TILE = 512
DSA_CHAIN_HEADS = 2
ROW_TILE = 256
PROJ_TILE = 512
VMEM_LIMIT_BYTES = 52 * 1024 * 1024

ROUTER_COL0 = N_GROUPS
INT_MIN = -2 ** 31
NEG_INF_KEY = -2139095041
M_INIT = -1e30
LOG2E = 1.4426950408889634
V_ROWS = 144
COUNT_ROWS = 64

_f32 = jnp.float32
_bf16 = jnp.bfloat16


def _params(n_axes):
    return pltpu.CompilerParams(dimension_semantics=("arbitrary",) * n_axes,
                                vmem_limit_bytes=VMEM_LIMIT_BYTES)


def _layer_norm(z, g, b):
    mu = jnp.mean(z, axis=-1, keepdims=True)
    zc = z - mu
    var = jnp.mean(zc * zc, axis=-1, keepdims=True)
    return zc * lax.rsqrt(var + LN_EPS) * g + b


def _col_max(s):
    rows = s.shape[0]
    if rows > COUNT_ROWS and rows % COUNT_ROWS == 0:
        s = jnp.max(s.reshape(rows // COUNT_ROWS, COUNT_ROWS, s.shape[1]), axis=0)
    return jnp.max(s, axis=0, keepdims=True)


def _head_slopes():
    return tuple(LOG2E * 2.0 ** (-8.0 * (h + 1) / N_HEADS) for h in range(N_HEADS))


def _proj_kernel(x_ref, *refs, scales):
    n = len(scales)
    x = x_ref[...].astype(_bf16)
    for w_ref, o_ref, s in zip(refs[:n], refs[n:], scales):
        y = jnp.dot(x, w_ref[...], preferred_element_type=_f32)
        if s != 1.0:
            y = y * s
        o_ref[...] = y.astype(o_ref.dtype)


def _project(x, weights, out_dtypes, scales):
    n_tok, k = x.shape
    tm = min(PROJ_TILE, n_tok)
    in_specs = [pl.BlockSpec((tm, k), lambda i: (i, 0))]
    in_specs += [pl.BlockSpec(w.shape, lambda i: (0, 0)) for w in weights]
    return pl.pallas_call(
        functools.partial(_proj_kernel, scales=tuple(scales)),
        grid=(n_tok // tm,),
        in_specs=in_specs,
        out_specs=[pl.BlockSpec((tm, w.shape[1]), lambda i: (i, 0)) for w in weights],
        out_shape=[jax.ShapeDtypeStruct((n_tok, w.shape[1]), dt) for w, dt in zip(weights, out_dtypes)],
        compiler_params=_params(1),
        name="proj",
    )(x, *weights)


def _kv_kernel(x_ref, wk_ref, wv_ref, k_ref, v_ref, km_ref):
    x = x_ref[...].astype(_bf16)
    k = jnp.dot(x, wk_ref[...], preferred_element_type=_f32)
    v = jnp.dot(x, wv_ref[...], preferred_element_type=_f32)
    k_ref[...] = k.astype(_bf16)
    v_ref[...] = v.astype(_bf16)
    km_ref[0] = jnp.mean(k, axis=0, keepdims=True)


def _shared_kv(x, wk, wv):
    n_tok, k = x.shape
    tm = MOBA_BLOCK
    return pl.pallas_call(
        _kv_kernel,
        grid=(n_tok // tm,),
        in_specs=[pl.BlockSpec((tm, k), lambda i: (i, 0)),
                  pl.BlockSpec(wk.shape, lambda i: (0, 0)),
                  pl.BlockSpec(wv.shape, lambda i: (0, 0))],
        out_specs=[pl.BlockSpec((tm, KV_DIM), lambda i: (i, 0)),
                   pl.BlockSpec((tm, KV_DIM), lambda i: (i, 0)),
                   pl.BlockSpec((1, 1, KV_DIM), lambda i: (i, 0, 0))],
        out_shape=[jax.ShapeDtypeStruct((n_tok, KV_DIM), _bf16),
                   jax.ShapeDtypeStruct((n_tok, KV_DIM), _bf16),
                   jax.ShapeDtypeStruct((n_tok // tm, 1, KV_DIM), _f32)],
        compiler_params=_params(1),
        name="shared_kv",
    )(x, wk, wv)


def _heads_to_lanes(a, bsz, seq, n_heads, dim):
    return a.reshape(bsz, seq, n_heads, dim).transpose(0, 2, 3, 1)


def _lanes_to_heads(a, bsz, seq):
    return a.transpose(0, 3, 1, 2).reshape(bsz * seq, -1)


def _dsa_kernel(qt_ref, iqt_ref, iwt_ref, ik_ref, k_ref, vt_ref, o_ref,
                keys_ref, qs_ref, tab_ref, m_ref, acc_ref, *, topk, seq, slopes):
    tq, tk = DSA_Q_TILE, DSA_K_TILE
    q0 = pl.program_id(1) * tq
    n_t = (q0 + tq + tk - 1) // tk
    qpos = q0 + lax.broadcasted_iota(jnp.int32, (1, tq), 1)
    kidx = lax.broadcasted_iota(jnp.int32, (tk, 1), 0)
    iw = iwt_ref[...] * (IDX_HEADS ** -0.5)

    def score_body(t, carry):
        ik = ik_ref[t]
        sc = jnp.zeros((tk, tq), _f32)
        for h in range(IDX_HEADS):
            s = jnp.dot(ik, iqt_ref[h], preferred_element_type=_f32)
            sc = sc + jnp.maximum(s * (IDX_DIM ** -0.5), 0.0) * iw[h:h + 1, :]
        kpos = t * tk + kidx
        sc = jnp.where(kpos <= qpos, sc, -jnp.inf)
        bits = pltpu.bitcast(sc, jnp.int32)
        keys = bits ^ ((bits >> 31) & 0x7FFFFFFF)
        keys_ref[t] = jnp.where(sc == 0.0, 0, keys)
        return carry

    lax.fori_loop(0, n_t, score_body, 0)

    def sublane_fold(m):
        return jnp.sum(m.reshape(tk // COUNT_ROWS, COUNT_ROWS, tq), axis=0)

    def count_ge(cand):
        def body(t, acc):
            return acc + sublane_fold(jnp.where(keys_ref[t] >= cand, 1.0, 0.0))
        acc = lax.fori_loop(0, n_t, body, jnp.zeros((COUNT_ROWS, tq), _f32))
        return jnp.sum(acc, axis=0, keepdims=True)

    def bit_body(i, tu):
        cand_u = tu | lax.shift_left(jnp.int32(1), 31 - i)
        cnt = count_ge(cand_u ^ INT_MIN)
        return jnp.where(cnt >= topk, cand_u, tu)

    tu = lax.fori_loop(0, 32, bit_body, jnp.zeros((1, tq), jnp.int32))
    thr = tu ^ INT_MIN
    c_ge = count_ge(thr)
    need = topk - count_ge(thr + 1)

    def tie_search():
        n_bits = (seq - 1).bit_length()

        def count_eq_before(pc):
            def body(t, acc):
                kpos = t * tk + kidx
                hit = jnp.where(keys_ref[t] == thr, jnp.where(kpos < pc, 1.0, 0.0), 0.0)
                return acc + sublane_fold(hit)
            acc = lax.fori_loop(0, n_t, body, jnp.zeros((COUNT_ROWS, tq), _f32))
            return jnp.sum(acc, axis=0, keepdims=True)

        def body(i, p):
            cand = p | lax.shift_left(jnp.int32(1), n_bits - 1 - i)
            return jnp.where(count_eq_before(cand) < need, cand, p)

        return lax.fori_loop(0, n_bits, body, jnp.zeros((1, tq), jnp.int32))

    excess = jnp.where(c_ge > topk, jnp.where(thr > NEG_INF_KEY, 1.0, 0.0), 0.0)
    pcut = lax.cond(jnp.max(excess) > 0.0, tie_search,
                    lambda: jnp.full((1, tq), seq, jnp.int32))

    n_chain = N_HEADS // DSA_CHAIN_HEADS
    chain_heads = [range(c * DSA_CHAIN_HEADS, (c + 1) * DSA_CHAIN_HEADS) for c in range(n_chain)]
    for c in range(n_chain):
        qs_ref[c] = jnp.concatenate([qt_ref[h] for h in chain_heads[c]], axis=1)
    kidx_f = lax.broadcasted_iota(jnp.int32, (tk, tq), 0).astype(_f32)
    for h in range(N_HEADS):
        tab_ref[h] = slopes[h] * kidx_f
    m_ref[...] = jnp.full(m_ref.shape, M_INIT, _f32)
    acc_ref[...] = jnp.zeros(acc_ref.shape, _f32)

    def att_body(t, carry):
        keys = keys_ref[t]
        kpos = t * tk + kidx
        take = jnp.where(keys > thr, 0.0,
                         jnp.where(keys == thr, jnp.where(kpos <= pcut, 0.0, -jnp.inf), -jnp.inf))
        take = jnp.where(kpos <= qpos, take, -jnp.inf)
        base = (t * tk - qpos).astype(_f32)
        s_all, p_all, scale_all = [], [], []
        for c in range(n_chain):
            g = chain_heads[c][0] // HEAD_GROUP
            s = jnp.dot(k_ref[t, :, g * HEAD_DIM:(g + 1) * HEAD_DIM], qs_ref[c],
                        preferred_element_type=_f32)
            s_all.append(s + jnp.concatenate([take + tab_ref[h] for h in chain_heads[c]], axis=1))
        for c in range(n_chain):
            off = jnp.concatenate([slopes[h] * base for h in chain_heads[c]], axis=1)
            m_prev = m_ref[c]
            m_new = jnp.maximum(m_prev, _col_max(s_all[c]) + off)
            p_all.append(jnp.exp2(s_all[c] - (m_new - off)).astype(_bf16))
            scale_all.append(jnp.exp2(m_prev - m_new))
            m_ref[c] = m_new
        for c in range(n_chain):
            g = chain_heads[c][0] // HEAD_GROUP
            acc_ref[c] = scale_all[c] * acc_ref[c] + jnp.dot(
                vt_ref[t, g * V_ROWS:(g + 1) * V_ROWS, :], p_all[c], preferred_element_type=_f32)
        return carry

    lax.fori_loop(0, n_t, att_body, 0)

    for c in range(n_chain):
        out = acc_ref[c, 0:HEAD_DIM, :] / acc_ref[c, HEAD_DIM:HEAD_DIM + 1, :]
        for j, h in enumerate(chain_heads[c]):
            o_ref[h] = out[:, j * tq:(j + 1) * tq].astype(o_ref.dtype)


def _values_to_lanes(v, lead):
    t = lead[-1]
    vt = v.reshape(*lead[:-1], t, N_KV_HEADS, HEAD_DIM)
    vt = jnp.moveaxis(vt, -3, -1)
    ones = jnp.ones(vt.shape[:-2] + (1, t), vt.dtype)
    zeros = jnp.zeros(vt.shape[:-2] + (V_ROWS - HEAD_DIM - 1, t), vt.dtype)
    vt = jnp.concatenate([vt, ones, zeros], axis=-2)
    return vt.reshape(*lead[:-1], N_KV_HEADS * V_ROWS, t)


def _dsa_attention(q, iq, ikw, k, v, bsz, seq):
    tq, tk = DSA_Q_TILE, DSA_K_TILE
    n_t = seq // tk
    topk = min(IDX_TOPK_MAX, seq // 4)
    qt = _heads_to_lanes(q, bsz, seq, N_HEADS, HEAD_DIM)
    iqt = _heads_to_lanes(iq.astype(_bf16), bsz, seq, IDX_HEADS, IDX_DIM)
    ikw3 = ikw.reshape(bsz, seq, LANES)
    ik = ikw3[:, :, :IDX_DIM].astype(_bf16).reshape(bsz, n_t, tk, IDX_DIM)
    iwt = jnp.pad(ikw3[:, :, IDX_DIM:IDX_DIM + IDX_HEADS].transpose(0, 2, 1),
                  ((0, 0), (0, 8 - IDX_HEADS), (0, 0)))
    k4 = k.reshape(bsz, n_t, tk, KV_DIM)
    vt = _values_to_lanes(v, (bsz, n_t, tk))
    n_chain = N_HEADS // DSA_CHAIN_HEADS
    rows = DSA_CHAIN_HEADS * tq
    out = pl.pallas_call(
        functools.partial(_dsa_kernel, topk=topk, seq=seq, slopes=_head_slopes()),
        grid=(bsz, seq // tq),
        in_specs=[pl.BlockSpec((None, N_HEADS, HEAD_DIM, tq), lambda b, i: (b, 0, 0, i)),
                  pl.BlockSpec((None, IDX_HEADS, IDX_DIM, tq), lambda b, i: (b, 0, 0, i)),
                  pl.BlockSpec((None, 8, tq), lambda b, i: (b, 0, i)),
                  pl.BlockSpec((None, n_t, tk, IDX_DIM), lambda b, i: (b, 0, 0, 0)),
                  pl.BlockSpec((None, n_t, tk, KV_DIM), lambda b, i: (b, 0, 0, 0)),
                  pl.BlockSpec((None, n_t, N_KV_HEADS * V_ROWS, tk), lambda b, i: (b, 0, 0, 0))],
        out_specs=pl.BlockSpec((None, N_HEADS, HEAD_DIM, tq), lambda b, i: (b, 0, 0, i)),
        out_shape=jax.ShapeDtypeStruct((bsz, N_HEADS, HEAD_DIM, seq), _bf16),
        scratch_shapes=[pltpu.VMEM((n_t, tk, tq), jnp.int32),
                        pltpu.VMEM((n_chain, HEAD_DIM, rows), _bf16),
                        pltpu.VMEM((N_HEADS, tk, tq), _f32),
                        pltpu.VMEM((n_chain, 1, rows), _f32),
                        pltpu.VMEM((n_chain, V_ROWS, rows), _f32)],
        compiler_params=_params(2),
        name="dsa_attention",
    )(qt, iqt, iwt, ik, k4, vt)
    return _lanes_to_heads(out, bsz, seq)


def _moba_kernel(qt_ref, km_ref, k_ref, vt_ref, o_ref,
                 sel_ref, m_ref, acc_ref, *, n_sel, slopes):
    blk = MOBA_BLOCK
    nbp = km_ref.shape[1]
    own = pl.program_id(1)
    ownf = own.astype(_f32)
    blkid = lax.broadcasted_iota(jnp.int32, (nbp, 1), 0).astype(_f32)
    qidx = lax.broadcasted_iota(jnp.int32, (1, blk), 1)
    kidx = lax.broadcasted_iota(jnp.int32, (blk, 1), 0)
    rel = (kidx - qidx).astype(_f32)

    for h in range(N_HEADS):
        g = h // HEAD_GROUP
        qh = qt_ref[h]
        dsl = slice(g * HEAD_DIM, (g + 1) * HEAD_DIM)
        gate = (jnp.dot(km_ref[0, :, dsl], qh, preferred_element_type=_f32)
                + jnp.dot(km_ref[1, :, dsl], qh, preferred_element_type=_f32))
        gl = jnp.where(blkid < ownf, gate, -jnp.inf)
        sel = jnp.zeros((nbp, blk), _f32)
        for _ in range(n_sel):
            mx = jnp.max(gl, axis=0, keepdims=True)
            cand = jnp.where(gl == mx, blkid, 1e9)
            cand = jnp.where(mx > -jnp.inf, cand, 1e9)
            pick = blkid == jnp.min(cand, axis=0, keepdims=True)
            sel = jnp.where(pick, 1.0, sel)
            gl = jnp.where(pick, -jnp.inf, gl)
        sel_ref[h] = sel

        s = jnp.dot(k_ref[own, :, dsl], qh, preferred_element_type=_f32)
        x = s + jnp.where(kidx <= qidx, slopes[h] * rel, -jnp.inf)
        m = jnp.max(x, axis=0, keepdims=True)
        p = jnp.exp2(x - m)
        m_ref[h] = m
        acc_ref[h] = jnp.dot(vt_ref[own, g * V_ROWS:(g + 1) * V_ROWS, :], p.astype(_bf16),
                             preferred_element_type=_f32)

    def blk_body(n, carry):
        shift = ((n - own) * blk).astype(_f32)
        s_all, p_all, scale_all = [], [], []
        for h in range(N_HEADS):
            g = h // HEAD_GROUP
            s_all.append(jnp.dot(k_ref[n, :, g * HEAD_DIM:(g + 1) * HEAD_DIM], qt_ref[h],
                                 preferred_element_type=_f32) + slopes[h] * rel)
        for h in range(N_HEADS):
            rowc = jnp.where(sel_ref[h, pl.ds(n, 1), :] > 0.0, slopes[h] * shift, -jnp.inf)
            m_prev = m_ref[h]
            m_new = jnp.maximum(m_prev, _col_max(s_all[h]) + rowc)
            p_all.append(jnp.exp2(s_all[h] - (m_new - rowc)).astype(_bf16))
            scale_all.append(jnp.exp2(m_prev - m_new))
            m_ref[h] = m_new
        for h in range(N_HEADS):
            g = h // HEAD_GROUP
            acc_ref[h] = scale_all[h] * acc_ref[h] + jnp.dot(
                vt_ref[n, g * V_ROWS:(g + 1) * V_ROWS, :], p_all[h], preferred_element_type=_f32)
        return carry

    lax.fori_loop(0, own, blk_body, 0)

    for h in range(N_HEADS):
        out = acc_ref[h, 0:HEAD_DIM, :] / acc_ref[h, HEAD_DIM:HEAD_DIM + 1, :]
        o_ref[h] = out.astype(o_ref.dtype)


def _moba_attention(q, k4, vt, km, bsz, seq):
    blk = MOBA_BLOCK
    n_blocks = seq // blk
    nbp = km.shape[2]
    n_sel = max(1, min(MOBA_TOPK, n_blocks - 1))
    out = pl.pallas_call(
        functools.partial(_moba_kernel, n_sel=n_sel, slopes=_head_slopes()),
        grid=(bsz, n_blocks),
        in_specs=[pl.BlockSpec((None, N_HEADS, HEAD_DIM, blk), lambda b, i: (b, 0, 0, i)),
                  pl.BlockSpec((None, 2, nbp, KV_DIM), lambda b, i: (b, 0, 0, 0)),
                  pl.BlockSpec((None, n_blocks, blk, KV_DIM), lambda b, i: (b, 0, 0, 0)),
                  pl.BlockSpec((None, n_blocks, N_KV_HEADS * V_ROWS, blk), lambda b, i: (b, 0, 0, 0))],
        out_specs=pl.BlockSpec((None, N_HEADS, HEAD_DIM, blk), lambda b, i: (b, 0, 0, i)),
        out_shape=jax.ShapeDtypeStruct((bsz, N_HEADS, HEAD_DIM, seq), _bf16),
        scratch_shapes=[pltpu.VMEM((N_HEADS, nbp, blk), _f32),
                        pltpu.VMEM((N_HEADS, 1, blk), _f32),
                        pltpu.VMEM((N_HEADS, V_ROWS, blk), _f32)],
        compiler_params=_params(2),
        name="moba_attention",
    )(_heads_to_lanes(q, bsz, seq, N_HEADS, HEAD_DIM), km, k4, vt)
    return _lanes_to_heads(out, bsz, seq)


def _mix_router_kernel(a_ref, wo_ref, h_ref, g_ref, b_ref, wr_ref, br_ref,
                       h1_ref, info_ref, cnt_ref, carry_ref, *, alpha):
    tm = a_ref.shape[0]

    @pl.when(pl.program_id(0) == 0)
    def _():
        carry_ref[...] = jnp.zeros(carry_ref.shape, _f32)

    mix = jnp.dot(a_ref[...], wo_ref[...], preferred_element_type=_f32)
    h1 = _layer_norm(alpha * h_ref[...] + mix, g_ref[...], b_ref[...])
    h1_ref[...] = h1

    logits = jnp.dot(h1, wr_ref[...], preferred_element_type=_f32,
                     precision=lax.Precision.HIGHEST) + br_ref[...]
    col = lax.broadcasted_iota(jnp.int32, (1, LANES), 1).astype(_f32)
    big = 1e9
    is_group = col < float(N_GROUPS)
    gl = jnp.where(is_group, logits, -jnp.inf)
    gmax = jnp.max(gl, axis=1, keepdims=True)
    gsel = jnp.min(jnp.where(gl == gmax, col, big), axis=1, keepdims=True)
    denom = jnp.sum(jnp.where(is_group, jnp.exp(logits - gmax), 0.0), axis=1, keepdims=True)
    g_w = 1.0 / denom

    lo = ROUTER_COL0 + EXPERTS_PER_GROUP * gsel
    in_group = jnp.where(col >= lo, jnp.where(col < lo + EXPERTS_PER_GROUP, 1.0, 0.0), 0.0)
    el = jnp.where(in_group > 0.0, logits, -jnp.inf)
    v1 = jnp.max(el, axis=1, keepdims=True)
    i1 = jnp.min(jnp.where(el == v1, col, big), axis=1, keepdims=True)
    el2 = jnp.where(col == i1, -jnp.inf, el)
    v2 = jnp.max(el2, axis=1, keepdims=True)
    i2 = jnp.min(jnp.where(el2 == v2, col, big), axis=1, keepdims=True)
    t = jnp.exp(v2 - v1)
    w1 = g_w * (1.0 / (1.0 + t))
    w2 = g_w * (t / (1.0 + t))

    o1 = jnp.where(col == i1, 1.0, 0.0)
    o2 = jnp.where(col == i2, 1.0, 0.0)
    both = o1 + o2
    ri = lax.broadcasted_iota(jnp.int32, (tm, tm), 0)
    ci = lax.broadcasted_iota(jnp.int32, (tm, tm), 1)
    earlier = jnp.where(ci < ri, 1.0, 0.0).astype(_bf16)
    before = jnp.dot(earlier, both.astype(_bf16), preferred_element_type=_f32) + carry_ref[...]
    r1 = jnp.sum(o1 * before, axis=1, keepdims=True)
    r2 = jnp.sum(o2 * before, axis=1, keepdims=True)
    carry_ref[...] = carry_ref[...] + jnp.sum(both, axis=0, keepdims=True)
    cnt_ref[...] = carry_ref[...]

    info = jnp.where(col == 0.0, i1 - ROUTER_COL0, 0.0)
    info = jnp.where(col == 1.0, i2 - ROUTER_COL0, info)
    info = jnp.where(col == 2.0, w1, info)
    info = jnp.where(col == 3.0, w2, info)
    info = jnp.where(col == 4.0, r1, info)
    info = jnp.where(col == 5.0, r2, info)
    info_ref[...] = info


def _mix_router(a, wo, h, ln_g, ln_b, wr, br, alpha):
    n_tok = h.shape[0]
    tm = min(ROW_TILE, n_tok)
    row = lambda i: (i, 0)
    fixed = lambda i: (0, 0)
    return pl.pallas_call(
        functools.partial(_mix_router_kernel, alpha=alpha),
        grid=(n_tok // tm,),
        in_specs=[pl.BlockSpec((tm, D_MODEL), row),
                  pl.BlockSpec((D_MODEL, D_MODEL), fixed),
                  pl.BlockSpec((tm, D_MODEL), row),
                  pl.BlockSpec((1, D_MODEL), fixed),
                  pl.BlockSpec((1, D_MODEL), fixed),
                  pl.BlockSpec((D_MODEL, LANES), fixed),
                  pl.BlockSpec((1, LANES), fixed)],
        out_specs=[pl.BlockSpec((tm, D_MODEL), row),
                   pl.BlockSpec((tm, LANES), row),
                   pl.BlockSpec((1, LANES), fixed)],
        out_shape=[jax.ShapeDtypeStruct((n_tok, D_MODEL), _f32),
                   jax.ShapeDtypeStruct((n_tok, LANES), _f32),
                   jax.ShapeDtypeStruct((1, LANES), _f32)],
        scratch_shapes=[pltpu.VMEM((1, LANES), _f32)],
        compiler_params=_params(1),
        name="mix_ln_router",
    )(a, wo, h, ln_g, ln_b, wr, br)


def _row_copy(src_ref, src_row, dst_ref, dst_row, sem):
    return pltpu.make_async_copy(src_ref.at[pl.ds(src_row, 1)], dst_ref.at[pl.ds(dst_row, 1)], sem)


def _scatter_kernel(dest_ref, x_ref, zero_ref, out_ref, sem):
    del zero_ref
    ts = x_ref.shape[0]

    def start(j, carry):
        for k in range(2):
            _row_copy(x_ref, j, out_ref, dest_ref[0, 0, 2 * j + k], sem).start()
        return carry

    lax.fori_loop(0, ts, start, 0)

    def wait(j, carry):
        _row_copy(x_ref, 0, out_ref, 0, sem).wait()
        return carry

    lax.fori_loop(0, 2 * ts, wait, 0)


def _moe_scatter(h1, dest, n_rows):
    n_tok = h1.shape[0]
    ts = min(ROW_TILE, n_tok)
    zeros = jnp.zeros((n_rows, D_MODEL), _f32)
    return pl.pallas_call(
        _scatter_kernel,
        grid=(n_tok // ts,),
        in_specs=[pl.BlockSpec((1, 1, 2 * ts), lambda i: (i, 0, 0), memory_space=pltpu.SMEM),
                  pl.BlockSpec((ts, D_MODEL), lambda i: (i, 0)),
                  pl.BlockSpec(memory_space=pl.ANY)],
        out_specs=pl.BlockSpec(memory_space=pl.ANY),
        out_shape=jax.ShapeDtypeStruct((n_rows, D_MODEL), _f32),
        scratch_shapes=[pltpu.SemaphoreType.DMA(())],
        input_output_aliases={2: 0},
        compiler_params=_params(1),
        name="moe_scatter",
    )(dest.reshape(n_tok // ts, 1, 2 * ts), h1, zeros)


def _ffn_kernel(be_ref, nu_ref, x_ref, wg_ref, wu_ref, wd_ref, y_ref):
    del be_ref
    i = pl.program_id(0)

    @pl.when(i < nu_ref[0])
    def _():
        x = x_ref[...].astype(_bf16)
        gate = jnp.dot(x, wg_ref[...], preferred_element_type=_f32)
        up = jnp.dot(x, wu_ref[...], preferred_element_type=_f32)
        act = gate * (1.0 / (1.0 + jnp.exp(-gate))) * up
        y_ref[...] = jnp.dot(act.astype(_bf16), wd_ref[...], preferred_element_type=_f32)

    @pl.when(i >= nu_ref[0])
    def _():
        y_ref[...] = jnp.zeros(y_ref.shape, _f32)


def _moe_ffn(rows, blk_expert, n_used, wg, wu, wd):
    n_rows = rows.shape[0]
    tb = ROW_TILE
    grid_spec = pltpu.PrefetchScalarGridSpec(
        num_scalar_prefetch=2,
        grid=(n_rows // tb,),
        in_specs=[pl.BlockSpec((tb, D_MODEL), lambda i, be, nu: (i, 0)),
                  pl.BlockSpec((None, D_MODEL, D_EXPERT), lambda i, be, nu: (be[i], 0, 0)),
                  pl.BlockSpec((None, D_MODEL, D_EXPERT), lambda i, be, nu: (be[i], 0, 0)),
                  pl.BlockSpec((None, D_EXPERT, D_MODEL), lambda i, be, nu: (be[i], 0, 0))],
        out_specs=pl.BlockSpec((tb, D_MODEL), lambda i, be, nu: (i, 0)),
    )
    return pl.pallas_call(
        _ffn_kernel,
        grid_spec=grid_spec,
        out_shape=jax.ShapeDtypeStruct((n_rows, D_MODEL), _f32),
        compiler_params=_params(1),
        name="moe_ffn",
    )(blk_expert, n_used, rows, wg, wu, wd)


def _combine_kernel(dest_ref, h_ref, info_ref, y_ref, g_ref, b_ref, o_ref, buf_ref, sem, *, alpha):
    ts = h_ref.shape[0]

    def start(j, carry):
        for k in range(2):
            _row_copy(y_ref, dest_ref[0, 0, 2 * j + k], buf_ref.at[k], j, sem).start()
        return carry

    lax.fori_loop(0, ts, start, 0)

    def wait(j, carry):
        _row_copy(y_ref, 0, buf_ref.at[0], 0, sem).wait()
        return carry

    lax.fori_loop(0, 2 * ts, wait, 0)

    ffn = buf_ref[0] * info_ref[:, 2:3] + buf_ref[1] * info_ref[:, 3:4]
    o_ref[...] = _layer_norm(alpha * h_ref[...] + ffn, g_ref[...], b_ref[...])


def _moe_combine(h1, info, y, dest, ln_g, ln_b, alpha):
    n_tok = h1.shape[0]
    ts = min(ROW_TILE, n_tok)
    row = lambda i: (i, 0)
    fixed = lambda i: (0, 0)
    return pl.pallas_call(
        functools.partial(_combine_kernel, alpha=alpha),
        grid=(n_tok // ts,),
        in_specs=[pl.BlockSpec((1, 1, 2 * ts), lambda i: (i, 0, 0), memory_space=pltpu.SMEM),
                  pl.BlockSpec((ts, D_MODEL), row),
                  pl.BlockSpec((ts, LANES), row),
                  pl.BlockSpec(memory_space=pl.ANY),
                  pl.BlockSpec((1, D_MODEL), fixed),
                  pl.BlockSpec((1, D_MODEL), fixed)],
        out_specs=pl.BlockSpec((ts, D_MODEL), row),
        out_shape=jax.ShapeDtypeStruct((n_tok, D_MODEL), _f32),
        scratch_shapes=[pltpu.VMEM((2, ts, D_MODEL), _f32), pltpu.SemaphoreType.DMA(())],
        compiler_params=_params(1),
        name="moe_combine_ln",
    )(dest.reshape(n_tok // ts, 1, 2 * ts), h1, info, y, ln_g, ln_b)


def _hier_moe(h1, info, counts, wg, wu, wd, ln_g, ln_b, alpha):
    n_tok = h1.shape[0]
    tb = ROW_TILE
    cnt = counts[0, ROUTER_COL0:ROUTER_COL0 + N_EXPERTS].astype(jnp.int32)
    padded = (cnt + tb - 1) // tb * tb
    pad_end = jnp.cumsum(padded)
    pad_start = pad_end - padded
    n_rows = (2 * n_tok + N_EXPERTS * (tb - 1) + tb - 1) // tb * tb
    expert = info[:, 0:2].astype(jnp.int32)
    rank = info[:, 4:6].astype(jnp.int32)
    dest = pad_start[expert] + rank
    blk_start = jnp.arange(n_rows // tb, dtype=jnp.int32) * tb
    blk_expert = jnp.minimum(jnp.sum(pad_end[None, :] <= blk_start[:, None], axis=1),
                             N_EXPERTS - 1).astype(jnp.int32)
    n_used = (pad_end[-1:] // tb).astype(jnp.int32)
    rows = _moe_scatter(h1, dest, n_rows)
    y = _moe_ffn(rows, blk_expert, n_used, wg, wu, wd)
    return _moe_combine(h1, info, y, dest, ln_g, ln_b, alpha)


def _ple_kernel(h_ref, p_ref, wp_ref, wg_ref, g_ref, b_ref, o_ref, *, alpha):
    h = h_ref[...]
    emb = jnp.dot(p_ref[...].astype(_bf16), wp_ref[...], preferred_element_type=_f32)
    gate = jnp.dot(h.astype(_bf16), wg_ref[...], preferred_element_type=_f32)
    ple = emb * (1.0 / (1.0 + jnp.exp(-gate)))
    o_ref[...] = _layer_norm(alpha * h + ple, g_ref[...], b_ref[...])


def _ple(h, p, wp, wg, ln_g, ln_b, alpha):
    n_tok = h.shape[0]
    tm = min(ROW_TILE, n_tok)
    row = lambda i: (i, 0)
    fixed = lambda i: (0, 0)
    return pl.pallas_call(
        functools.partial(_ple_kernel, alpha=alpha),
        grid=(n_tok // tm,),
        in_specs=[pl.BlockSpec((tm, D_MODEL), row),
                  pl.BlockSpec((tm, p.shape[1]), row),
                  pl.BlockSpec(wp.shape, fixed),
                  pl.BlockSpec(wg.shape, fixed),
                  pl.BlockSpec((1, D_MODEL), fixed),
                  pl.BlockSpec((1, D_MODEL), fixed)],
        out_specs=pl.BlockSpec((tm, D_MODEL), row),
        out_shape=jax.ShapeDtypeStruct((n_tok, D_MODEL), _f32),
        compiler_params=_params(1),
        name="ple_ln",
    )(h, p, wp, wg, ln_g, ln_b)


def kernel(x, p, w_in_a, w_o_a, w_kv_shared, w_q_b, w_o_b, w_router_group, b_router_group,
           w_router_expert, b_router_expert, w_gate, w_up, w_down, w_ple, w_ple_gate, ln_g, ln_b):
    bsz, seq, _ = x.shape
    depth = w_gate.shape[0]
    n_a = w_in_a.shape[0]
    alpha = (2.0 * depth) ** 0.25
    n_tok = bsz * seq
    assert seq % DSA_K_TILE == 0 and seq % MOBA_BLOCK == 0 and n_tok % ROW_TILE == 0
    scale = LOG2E * HEAD_DIM ** -0.5
    c_k = D_MODEL + KV_DIM
    c_v = D_MODEL + 2 * KV_DIM
    c_iq = c_v + IDX_HEADS * IDX_DIM

    h = x.reshape(n_tok, D_MODEL)
    k4 = vt = km = None
    for i in range(depth):
        if i < n_a:
            w = w_in_a[i]
            w_ikw = jnp.pad(w[:, c_iq:], ((0, 0), (0, LANES - (w.shape[1] - c_iq))))
            q, k, v, iq, ikw = _project(
                h,
                [w[:, :D_MODEL].astype(_bf16), w[:, D_MODEL:c_k].astype(_bf16),
                 w[:, c_k:c_v].astype(_bf16), w[:, c_v:c_iq].astype(_bf16), w_ikw.astype(_bf16)],
                [_bf16, _bf16, _bf16, _f32, _f32],
                [scale, 1.0, 1.0, 1.0, 1.0])
            attn = _dsa_attention(q, iq, ikw, k, v, bsz, seq)
            w_o = w_o_a[i]
        else:
            j = i - n_a
            if k4 is None:
                n_blocks = seq // MOBA_BLOCK
                nbp = -(-n_blocks // 16) * 16
                k, v, km = _shared_kv(h, w_kv_shared[:, :KV_DIM].astype(_bf16),
                                      w_kv_shared[:, KV_DIM:].astype(_bf16))
                k4 = k.reshape(bsz, n_blocks, MOBA_BLOCK, KV_DIM)
                vt = _values_to_lanes(v, (bsz, n_blocks, MOBA_BLOCK))
                km = jnp.pad(km.reshape(bsz, n_blocks, KV_DIM), ((0, 0), (0, nbp - n_blocks), (0, 0)))
                km_hi = km.astype(_bf16)
                km_lo = (km - km_hi.astype(_f32)).astype(_bf16)
                km = jnp.stack([km_hi, km_lo], axis=1)
            (q,) = _project(h, [w_q_b[j].astype(_bf16)], [_bf16], [scale])
            attn = _moba_attention(q, k4, vt, km, bsz, seq)
            w_o = w_o_b[j]

        w_r = jnp.concatenate([w_router_group[i], w_router_expert[i]], axis=1)
        w_r = jnp.pad(w_r, ((0, 0), (0, LANES - w_r.shape[1])))
        b_r = jnp.concatenate([b_router_group[i], b_router_expert[i]])
        b_r = jnp.pad(b_r, (0, LANES - b_r.shape[0])).reshape(1, LANES)
        h1, info, counts = _mix_router(attn, w_o.astype(_bf16), h, ln_g[i, 0].reshape(1, -1),
                                       ln_b[i, 0].reshape(1, -1), w_r, b_r, alpha)
        h2 = _hier_moe(h1, info, counts, w_gate[i].astype(_bf16), w_up[i].astype(_bf16),
                       w_down[i].astype(_bf16), ln_g[i, 1].reshape(1, -1), ln_b[i, 1].reshape(1, -1),
                       alpha)
        h = _ple(h2, p[i].reshape(n_tok, -1), w_ple[i].astype(_bf16), w_ple_gate[i].astype(_bf16),
                 ln_g[i, 2].reshape(1, -1), ln_b[i, 2].reshape(1, -1), alpha)
    return h.reshape(bsz, seq, D_MODEL)
```

```python
import functools

import jax
import jax.numpy as jnp
from jax import lax
from jax.experimental import pallas as pl
from jax.experimental.pallas import tpu as pltpu

D_MODEL = 1024
N_HEADS = 8
HEAD_DIM = 128
N_KV_HEADS = 2
HEAD_GROUP = N_HEADS // N_KV_HEADS
KV_DIM = N_KV_HEADS * HEAD_DIM
IDX_HEADS = 4
IDX_DIM = 64
IDX_TOPK_MAX = 256
MOBA_BLOCK = 256
MOBA_TOPK = 3
N_GROUPS = 4
EXPERTS_PER_GROUP = 8
N_EXPERTS = N_GROUPS * EXPERTS_PER_GROUP
D_EXPERT = 512
LN_EPS = 1e-5

LANES = 128
DSA_Q_TILE = 128
DSA_K_TILE = 512
DSA_CHAIN_HEADS = 2
ROW_TILE = 256
PROJ_TILE = 512
ROW_DMA_UNROLL = 8
VMEM_LIMIT_BYTES = 52 * 1024 * 1024

ROUTER_COL0 = N_GROUPS
INT_MIN = -2 ** 31
NEG_INF_KEY = -2139095041
M_INIT = -1e30
LOG2E = 1.4426950408889634
V_ROWS = 144
COUNT_ROWS = 64

_f32 = jnp.float32
_bf16 = jnp.bfloat16


def _params(n_axes):
    return pltpu.CompilerParams(dimension_semantics=("arbitrary",) * n_axes,
                                vmem_limit_bytes=VMEM_LIMIT_BYTES)


def _layer_norm(z, g, b):
    mu = jnp.mean(z, axis=-1, keepdims=True)
    zc = z - mu
    var = jnp.mean(zc * zc, axis=-1, keepdims=True)
    return zc * lax.rsqrt(var + LN_EPS) * g + b


def _col_max(s):
    rows = s.shape[0]
    if rows > COUNT_ROWS and rows % COUNT_ROWS == 0:
        s = jnp.max(s.reshape(rows // COUNT_ROWS, COUNT_ROWS, s.shape[1]), axis=0)
    return jnp.max(s, axis=0, keepdims=True)


def _head_slopes():
    return tuple(LOG2E * 2.0 ** (-8.0 * (h + 1) / N_HEADS) for h in range(N_HEADS))


def _proj_kernel(x_ref, *refs, scales):
    n = len(scales)
    x = x_ref[...].astype(_bf16)
    for w_ref, o_ref, s in zip(refs[:n], refs[n:], scales):
        y = jnp.dot(x, w_ref[...], preferred_element_type=_f32)
        if s != 1.0:
            y = y * s
        o_ref[...] = y.astype(o_ref.dtype)


def _project(x, weights, out_dtypes, scales):
    n_tok, k = x.shape
    tm = min(PROJ_TILE, n_tok)
    in_specs = [pl.BlockSpec((tm, k), lambda i: (i, 0))]
    in_specs += [pl.BlockSpec(w.shape, lambda i: (0, 0)) for w in weights]
    return pl.pallas_call(
        functools.partial(_proj_kernel, scales=tuple(scales)),
        grid=(n_tok // tm,),
        in_specs=in_specs,
        out_specs=[pl.BlockSpec((tm, w.shape[1]), lambda i: (i, 0)) for w in weights],
        out_shape=[jax.ShapeDtypeStruct((n_tok, w.shape[1]), dt) for w, dt in zip(weights, out_dtypes)],
        compiler_params=_params(1),
        name="proj",
    )(x, *weights)


def _kv_kernel(x_ref, wk_ref, wv_ref, k_ref, v_ref, km_ref):
    x = x_ref[...].astype(_bf16)
    k = jnp.dot(x, wk_ref[...], preferred_element_type=_f32)
    v = jnp.dot(x, wv_ref[...], preferred_element_type=_f32)
    k_ref[...] = k.astype(_bf16)
    v_ref[...] = v.astype(_bf16)
    km_ref[0] = jnp.mean(k, axis=0, keepdims=True)


def _shared_kv(x, wk, wv):
    n_tok, k = x.shape
    tm = MOBA_BLOCK
    return pl.pallas_call(
        _kv_kernel,
        grid=(n_tok // tm,),
        in_specs=[pl.BlockSpec((tm, k), lambda i: (i, 0)),
                  pl.BlockSpec(wk.shape, lambda i: (0, 0)),
                  pl.BlockSpec(wv.shape, lambda i: (0, 0))],
        out_specs=[pl.BlockSpec((tm, KV_DIM), lambda i: (i, 0)),
                   pl.BlockSpec((tm, KV_DIM), lambda i: (i, 0)),
                   pl.BlockSpec((1, 1, KV_DIM), lambda i: (i, 0, 0))],
        out_shape=[jax.ShapeDtypeStruct((n_tok, KV_DIM), _bf16),
                   jax.ShapeDtypeStruct((n_tok, KV_DIM), _bf16),
                   jax.ShapeDtypeStruct((n_tok // tm, 1, KV_DIM), _f32)],
        compiler_params=_params(1),
        name="shared_kv",
    )(x, wk, wv)


def _heads_to_lanes(a, bsz, seq, n_heads, dim):
    return a.reshape(bsz, seq, n_heads, dim).transpose(0, 2, 3, 1)


def _lanes_to_heads(a, bsz, seq):
    return a.transpose(0, 3, 1, 2).reshape(bsz * seq, -1)


def _dsa_kernel(qt_ref, iqt_ref, iwt_ref, ik_ref, k_ref, vt_ref, o_ref,
                keys_ref, half_ref, qs_ref, tab_ref, m_ref, acc_ref, *, topk, seq, slopes):
    tq, tk = DSA_Q_TILE, DSA_K_TILE
    q0 = pl.program_id(1) * tq
    n_t = (q0 + tq + tk - 1) // tk
    qpos = q0 + lax.broadcasted_iota(jnp.int32, (1, tq), 1)
    kidx = lax.broadcasted_iota(jnp.int32, (tk, 1), 0)
    iw = iwt_ref[...] * (IDX_HEADS ** -0.5)

    def score_body(t, carry):
        ik = ik_ref[t]
        sc = jnp.zeros((tk, tq), _f32)
        for h in range(IDX_HEADS):
            s = jnp.dot(ik, iqt_ref[h], preferred_element_type=_f32)
            sc = sc + jnp.maximum(s * (IDX_DIM ** -0.5), 0.0) * iw[h:h + 1, :]
        kpos = t * tk + kidx
        sc = jnp.where(kpos <= qpos, sc, -jnp.inf)
        bits = pltpu.bitcast(sc, jnp.int32)
        keys = bits ^ ((bits >> 31) & 0x7FFFFFFF)
        keys = jnp.where(sc == 0.0, 0, keys)
        keys_ref[t] = keys
        half_ref[t] = (keys >> 16).astype(jnp.int16)
        return carry

    lax.fori_loop(0, n_t, score_body, 0)

    def sublane_fold(m):
        part = m[0:COUNT_ROWS]
        for c in range(1, tk // COUNT_ROWS):
            part = part + m[c * COUNT_ROWS:(c + 1) * COUNT_ROWS]
        return part

    def count_ge(cand):
        def body(t, acc):
            return acc + sublane_fold(jnp.where(keys_ref[t] >= cand, 1.0, 0.0))
        acc = lax.fori_loop(0, n_t, body, jnp.zeros((COUNT_ROWS, tq), _f32))
        return jnp.sum(acc, axis=0, keepdims=True)

    one16, zero16 = jnp.int16(1), jnp.int16(0)

    def count_half_ge(cand):
        def body(t, acc):
            return acc + sublane_fold(jnp.where(half_ref[t] >= cand, one16, zero16))
        acc = lax.fori_loop(0, n_t, body, jnp.zeros((COUNT_ROWS, tq), jnp.int16))
        return jnp.sum(acc.astype(_f32), axis=0, keepdims=True)

    def search_half(want):
        def bit_body(i, tu):
            cand_u = tu | lax.shift_left(jnp.int32(1), 15 - i)
            cnt = count_half_ge((cand_u - 2 ** 15).astype(jnp.int16))
            return jnp.where(cnt >= want, cand_u, tu)
        return lax.fori_loop(0, 16, bit_body, jnp.zeros((1, tq), jnp.int32))

    thr_hi = search_half(topk) - 2 ** 15
    above = jnp.where(thr_hi < 2 ** 15 - 1,
                      count_half_ge(jnp.minimum(thr_hi + 1, 2 ** 15 - 1).astype(jnp.int16)), 0.0)

    def low_body(t, carry):
        keys = keys_ref[t]
        low = jnp.where((keys >> 16) == thr_hi, (keys & 0xFFFF) - 2 ** 15, -2 ** 15)
        half_ref[t] = low.astype(jnp.int16)
        return carry

    lax.fori_loop(0, n_t, low_body, 0)
    thr = lax.shift_left(thr_hi, 16) | search_half(topk - above)
    c_ge = count_ge(thr)
    need = topk - count_ge(thr + 1)

    def tie_search():
        n_bits = (seq - 1).bit_length()

        def count_eq_before(pc):
            def body(t, acc):
                kpos = t * tk + kidx
                hit = jnp.where(keys_ref[t] == thr, jnp.where(kpos < pc, 1.0, 0.0), 0.0)
                return acc + sublane_fold(hit)
            acc = lax.fori_loop(0, n_t, body, jnp.zeros((COUNT_ROWS, tq), _f32))
            return jnp.sum(acc, axis=0, keepdims=True)

        def body(i, p):
            cand = p | lax.shift_left(jnp.int32(1), n_bits - 1 - i)
            return jnp.where(count_eq_before(cand) < need, cand, p)

        return lax.fori_loop(0, n_bits, body, jnp.zeros((1, tq), jnp.int32))

    excess = jnp.where(c_ge > topk, jnp.where(thr > NEG_INF_KEY, 1.0, 0.0), 0.0)
    pcut = lax.cond(jnp.max(excess) > 0.0, tie_search,
                    lambda: jnp.full((1, tq), seq, jnp.int32))

    n_chain = N_HEADS // DSA_CHAIN_HEADS
    chain_heads = [range(c * DSA_CHAIN_HEADS, (c + 1) * DSA_CHAIN_HEADS) for c in range(n_chain)]
    for c in range(n_chain):
        qs_ref[c] = jnp.concatenate([qt_ref[h] for h in chain_heads[c]], axis=1)
    kidx_f = lax.broadcasted_iota(jnp.int32, (tk, tq), 0).astype(_f32)
    for h in range(N_HEADS):
        tab_ref[h] = slopes[h] * kidx_f
    m_ref[...] = jnp.full(m_ref.shape, M_INIT, _f32)
    acc_ref[...] = jnp.zeros(acc_ref.shape, _f32)

    def att_body(t, carry):
        keys = keys_ref[t]
        kpos = t * tk + kidx
        take = jnp.where(keys > thr, 0.0,
                         jnp.where(keys == thr, jnp.where(kpos <= pcut, 0.0, -jnp.inf), -jnp.inf))
        take = jnp.where(kpos <= qpos, take, -jnp.inf)
        base = (t * tk - qpos).astype(_f32)
        s_all, p_all, scale_all = [], [], []
        for c in range(n_chain):
            g = chain_heads[c][0] // HEAD_GROUP
            s = jnp.dot(k_ref[t, :, g * HEAD_DIM:(g + 1) * HEAD_DIM], qs_ref[c],
                        preferred_element_type=_f32)
            s_all.append(s + jnp.concatenate([take + tab_ref[h] for h in chain_heads[c]], axis=1))
        for c in range(n_chain):
            off = jnp.concatenate([slopes[h] * base for h in chain_heads[c]], axis=1)
            m_prev = m_ref[c]
            m_new = jnp.maximum(m_prev, _col_max(s_all[c]) + off)
            p_all.append(jnp.exp2(s_all[c] - (m_new - off)).astype(_bf16))
            scale_all.append(jnp.exp2(m_prev - m_new))
            m_ref[c] = m_new
        for c in range(n_chain):
            g = chain_heads[c][0] // HEAD_GROUP
            acc_ref[c] = scale_all[c] * acc_ref[c] + jnp.dot(
                vt_ref[t, g * V_ROWS:(g + 1) * V_ROWS, :], p_all[c], preferred_element_type=_f32)
        return carry

    lax.fori_loop(0, n_t, att_body, 0)

    for c in range(n_chain):
        out = acc_ref[c, 0:HEAD_DIM, :] / acc_ref[c, HEAD_DIM:HEAD_DIM + 1, :]
        for j, h in enumerate(chain_heads[c]):
            o_ref[h] = out[:, j * tq:(j + 1) * tq].astype(o_ref.dtype)


def _values_to_lanes(v, lead):
    t = lead[-1]
    vt = v.reshape(*lead[:-1], t, N_KV_HEADS, HEAD_DIM)
    vt = jnp.moveaxis(vt, -3, -1)
    ones = jnp.ones(vt.shape[:-2] + (1, t), vt.dtype)
    zeros = jnp.zeros(vt.shape[:-2] + (V_ROWS - HEAD_DIM - 1, t), vt.dtype)
    vt = jnp.concatenate([vt, ones, zeros], axis=-2)
    return vt.reshape(*lead[:-1], N_KV_HEADS * V_ROWS, t)


def _dsa_attention(q, iq, ikw, k, v, bsz, seq):
    tq, tk = DSA_Q_TILE, DSA_K_TILE
    n_t = seq // tk
    topk = min(IDX_TOPK_MAX, seq // 4)
    qt = _heads_to_lanes(q, bsz, seq, N_HEADS, HEAD_DIM)
    iqt = _heads_to_lanes(iq.astype(_bf16), bsz, seq, IDX_HEADS, IDX_DIM)
    ikw3 = ikw.reshape(bsz, seq, LANES)
    ik = ikw3[:, :, :IDX_DIM].astype(_bf16).reshape(bsz, n_t, tk, IDX_DIM)
    iwt = jnp.pad(ikw3[:, :, IDX_DIM:IDX_DIM + IDX_HEADS].transpose(0, 2, 1),
                  ((0, 0), (0, 8 - IDX_HEADS), (0, 0)))
    k4 = k.reshape(bsz, n_t, tk, KV_DIM)
    vt = _values_to_lanes(v, (bsz, n_t, tk))
    n_chain = N_HEADS // DSA_CHAIN_HEADS
    rows = DSA_CHAIN_HEADS * tq
    out = pl.pallas_call(
        functools.partial(_dsa_kernel, topk=topk, seq=seq, slopes=_head_slopes()),
        grid=(bsz, seq // tq),
        in_specs=[pl.BlockSpec((None, N_HEADS, HEAD_DIM, tq), lambda b, i: (b, 0, 0, i)),
                  pl.BlockSpec((None, IDX_HEADS, IDX_DIM, tq), lambda b, i: (b, 0, 0, i)),
                  pl.BlockSpec((None, 8, tq), lambda b, i: (b, 0, i)),
                  pl.BlockSpec((None, n_t, tk, IDX_DIM), lambda b, i: (b, 0, 0, 0)),
                  pl.BlockSpec((None, n_t, tk, KV_DIM), lambda b, i: (b, 0, 0, 0)),
                  pl.BlockSpec((None, n_t, N_KV_HEADS * V_ROWS, tk), lambda b, i: (b, 0, 0, 0))],
        out_specs=pl.BlockSpec((None, N_HEADS, HEAD_DIM, tq), lambda b, i: (b, 0, 0, i)),
        out_shape=jax.ShapeDtypeStruct((bsz, N_HEADS, HEAD_DIM, seq), _bf16),
        scratch_shapes=[pltpu.VMEM((n_t, tk, tq), jnp.int32),
                        pltpu.VMEM((n_t, tk, tq), jnp.int16),
                        pltpu.VMEM((n_chain, HEAD_DIM, rows), _bf16),
                        pltpu.VMEM((N_HEADS, tk, tq), _f32),
                        pltpu.VMEM((n_chain, 1, rows), _f32),
                        pltpu.VMEM((n_chain, V_ROWS, rows), _f32)],
        compiler_params=_params(2),
        name="dsa_attention",
    )(qt, iqt, iwt, ik, k4, vt)
    return _lanes_to_heads(out, bsz, seq)


def _moba_kernel(qt_ref, km_ref, k_ref, vt_ref, o_ref,
                 sel_ref, m_ref, acc_ref, *, n_sel, slopes):
    blk = MOBA_BLOCK
    nbp = km_ref.shape[1]
    own = pl.program_id(1)
    ownf = own.astype(_f32)
    blkid = lax.broadcasted_iota(jnp.int32, (nbp, 1), 0).astype(_f32)
    qidx = lax.broadcasted_iota(jnp.int32, (1, blk), 1)
    kidx = lax.broadcasted_iota(jnp.int32, (blk, 1), 0)
    rel = (kidx - qidx).astype(_f32)

    gates, scores, probs = [], [], []
    for h in range(N_HEADS):
        dsl = slice(h // HEAD_GROUP * HEAD_DIM, (h // HEAD_GROUP + 1) * HEAD_DIM)
        gates.append(jnp.dot(km_ref[0, :, dsl], qt_ref[h], preferred_element_type=_f32)
                     + jnp.dot(km_ref[1, :, dsl], qt_ref[h], preferred_element_type=_f32))
        scores.append(jnp.dot(k_ref[own, :, dsl], qt_ref[h], preferred_element_type=_f32))
    for h in range(N_HEADS):
        gl = jnp.where(blkid < ownf, gates[h], -jnp.inf)
        sel = jnp.zeros((nbp, blk), _f32)
        for _ in range(n_sel):
            mx = jnp.max(gl, axis=0, keepdims=True)
            cand = jnp.where(gl == mx, blkid, 1e9)
            cand = jnp.where(mx > -jnp.inf, cand, 1e9)
            pick = blkid == jnp.min(cand, axis=0, keepdims=True)
            sel = jnp.where(pick, 1.0, sel)
            gl = jnp.where(pick, -jnp.inf, gl)
        sel_ref[h] = sel
    for h in range(N_HEADS):
        x = scores[h] + jnp.where(kidx <= qidx, slopes[h] * rel, -jnp.inf)
        m = _col_max(x)
        probs.append(jnp.exp2(x - m).astype(_bf16))
        m_ref[h] = m
    for h in range(N_HEADS):
        g = h // HEAD_GROUP
        acc_ref[h] = jnp.dot(vt_ref[own, g * V_ROWS:(g + 1) * V_ROWS, :], probs[h],
                             preferred_element_type=_f32)

    def blk_body(n, carry):
        shift = ((n - own) * blk).astype(_f32)
        s_all, p_all, scale_all = [], [], []
        for h in range(N_HEADS):
            g = h // HEAD_GROUP
            s_all.append(jnp.dot(k_ref[n, :, g * HEAD_DIM:(g + 1) * HEAD_DIM], qt_ref[h],
                                 preferred_element_type=_f32) + slopes[h] * rel)
        for h in range(N_HEADS):
            rowc = jnp.where(sel_ref[h, pl.ds(n, 1), :] > 0.0, slopes[h] * shift, -jnp.inf)
            m_prev = m_ref[h]
            m_new = jnp.maximum(m_prev, _col_max(s_all[h]) + rowc)
            p_all.append(jnp.exp2(s_all[h] - (m_new - rowc)).astype(_bf16))
            scale_all.append(jnp.exp2(m_prev - m_new))
            m_ref[h] = m_new
        for h in range(N_HEADS):
            g = h // HEAD_GROUP
            acc_ref[h] = scale_all[h] * acc_ref[h] + jnp.dot(
                vt_ref[n, g * V_ROWS:(g + 1) * V_ROWS, :], p_all[h], preferred_element_type=_f32)
        return carry

    lax.fori_loop(0, own, blk_body, 0)

    for h in range(N_HEADS):
        out = acc_ref[h, 0:HEAD_DIM, :] / acc_ref[h, HEAD_DIM:HEAD_DIM + 1, :]
        o_ref[h] = out.astype(o_ref.dtype)


def _moba_attention(q, k4, vt, km, bsz, seq):
    blk = MOBA_BLOCK
    n_blocks = seq // blk
    nbp = km.shape[2]
    n_sel = max(1, min(MOBA_TOPK, n_blocks - 1))
    out = pl.pallas_call(
        functools.partial(_moba_kernel, n_sel=n_sel, slopes=_head_slopes()),
        grid=(bsz, n_blocks),
        in_specs=[pl.BlockSpec((None, N_HEADS, HEAD_DIM, blk), lambda b, i: (b, 0, 0, i)),
                  pl.BlockSpec((None, 2, nbp, KV_DIM), lambda b, i: (b, 0, 0, 0)),
                  pl.BlockSpec((None, n_blocks, blk, KV_DIM), lambda b, i: (b, 0, 0, 0)),
                  pl.BlockSpec((None, n_blocks, N_KV_HEADS * V_ROWS, blk), lambda b, i: (b, 0, 0, 0))],
        out_specs=pl.BlockSpec((None, N_HEADS, HEAD_DIM, blk), lambda b, i: (b, 0, 0, i)),
        out_shape=jax.ShapeDtypeStruct((bsz, N_HEADS, HEAD_DIM, seq), _bf16),
        scratch_shapes=[pltpu.VMEM((N_HEADS, nbp, blk), _f32),
                        pltpu.VMEM((N_HEADS, 1, blk), _f32),
                        pltpu.VMEM((N_HEADS, V_ROWS, blk), _f32)],
        compiler_params=_params(2),
        name="moba_attention",
    )(_heads_to_lanes(q, bsz, seq, N_HEADS, HEAD_DIM), km, k4, vt)
    return _lanes_to_heads(out, bsz, seq)


def _mix_router_kernel(a_ref, wo_ref, h_ref, g_ref, b_ref, wr_ref, br_ref,
                       h1_ref, info_ref, cnt_ref, carry_ref, *, alpha):
    tm = a_ref.shape[0]

    @pl.when(pl.program_id(0) == 0)
    def _():
        carry_ref[...] = jnp.zeros(carry_ref.shape, _f32)

    mix = jnp.dot(a_ref[...], wo_ref[...], preferred_element_type=_f32)
    h1 = _layer_norm(alpha * h_ref[...] + mix, g_ref[...], b_ref[...])
    h1_ref[...] = h1

    h1_hi = h1.astype(_bf16)
    h1_lo = (h1 - h1_hi.astype(_f32)).astype(_bf16)
    logits = (jnp.dot(h1_hi, wr_ref[0], preferred_element_type=_f32)
              + jnp.dot(h1_lo, wr_ref[0], preferred_element_type=_f32)
              + jnp.dot(h1_hi, wr_ref[1], preferred_element_type=_f32)) + br_ref[...]
    col = lax.broadcasted_iota(jnp.int32, (1, LANES), 1).astype(_f32)
    big = 1e9
    is_group = col < float(N_GROUPS)
    gl = jnp.where(is_group, logits, -jnp.inf)
    gmax = jnp.max(gl, axis=1, keepdims=True)
    gsel = jnp.min(jnp.where(gl == gmax, col, big), axis=1, keepdims=True)
    denom = jnp.sum(jnp.where(is_group, jnp.exp(logits - gmax), 0.0), axis=1, keepdims=True)
    g_w = 1.0 / denom

    lo = ROUTER_COL0 + EXPERTS_PER_GROUP * gsel
    in_group = jnp.where(col >= lo, jnp.where(col < lo + EXPERTS_PER_GROUP, 1.0, 0.0), 0.0)
    el = jnp.where(in_group > 0.0, logits, -jnp.inf)
    v1 = jnp.max(el, axis=1, keepdims=True)
    i1 = jnp.min(jnp.where(el == v1, col, big), axis=1, keepdims=True)
    el2 = jnp.where(col == i1, -jnp.inf, el)
    v2 = jnp.max(el2, axis=1, keepdims=True)
    i2 = jnp.min(jnp.where(el2 == v2, col, big), axis=1, keepdims=True)
    t = jnp.exp(v2 - v1)
    w1 = g_w * (1.0 / (1.0 + t))
    w2 = g_w * (t / (1.0 + t))

    o1 = jnp.where(col == i1, 1.0, 0.0)
    o2 = jnp.where(col == i2, 1.0, 0.0)
    both = o1 + o2
    ri = lax.broadcasted_iota(jnp.int32, (tm, tm), 0)
    ci = lax.broadcasted_iota(jnp.int32, (tm, tm), 1)
    earlier = jnp.where(ci < ri, 1.0, 0.0).astype(_bf16)
    before = jnp.dot(earlier, both.astype(_bf16), preferred_element_type=_f32) + carry_ref[...]
    r1 = jnp.sum(o1 * before, axis=1, keepdims=True)
    r2 = jnp.sum(o2 * before, axis=1, keepdims=True)
    carry_ref[...] = carry_ref[...] + jnp.sum(both, axis=0, keepdims=True)
    cnt_ref[...] = carry_ref[...]

    info = jnp.where(col == 0.0, i1 - ROUTER_COL0, 0.0)
    info = jnp.where(col == 1.0, i2 - ROUTER_COL0, info)
    info = jnp.where(col == 2.0, w1, info)
    info = jnp.where(col == 3.0, w2, info)
    info = jnp.where(col == 4.0, r1, info)
    info = jnp.where(col == 5.0, r2, info)
    info_ref[...] = info


def _mix_router(a, wo, h, ln_g, ln_b, wr, br, alpha):
    n_tok = h.shape[0]
    tm = min(ROW_TILE, n_tok)
    row = lambda i: (i, 0)
    fixed = lambda i: (0, 0)
    return pl.pallas_call(
        functools.partial(_mix_router_kernel, alpha=alpha),
        grid=(n_tok // tm,),
        in_specs=[pl.BlockSpec((tm, D_MODEL), row),
                  pl.BlockSpec((D_MODEL, D_MODEL), fixed),
                  pl.BlockSpec((tm, D_MODEL), row),
                  pl.BlockSpec((1, D_MODEL), fixed),
                  pl.BlockSpec((1, D_MODEL), fixed),
                  pl.BlockSpec((2, D_MODEL, LANES), lambda i: (0, 0, 0)),
                  pl.BlockSpec((1, LANES), fixed)],
        out_specs=[pl.BlockSpec((tm, D_MODEL), row),
                   pl.BlockSpec((tm, LANES), row),
                   pl.BlockSpec((1, LANES), fixed)],
        out_shape=[jax.ShapeDtypeStruct((n_tok, D_MODEL), _f32),
                   jax.ShapeDtypeStruct((n_tok, LANES), _f32),
                   jax.ShapeDtypeStruct((1, LANES), _f32)],
        scratch_shapes=[pltpu.VMEM((1, LANES), _f32)],
        compiler_params=_params(1),
        name="mix_ln_router",
    )(a, wo, h, ln_g, ln_b, wr, br)


def _row_copy(src_ref, src_row, dst_ref, dst_row, sem):
    return pltpu.make_async_copy(src_ref.at[pl.ds(src_row, 1)], dst_ref.at[pl.ds(dst_row, 1)], sem)


def _rows_copy(src_ref, dst_ref, n_rows, sem):
    return pltpu.make_async_copy(src_ref.at[pl.ds(0, n_rows)], dst_ref.at[pl.ds(0, n_rows)], sem)


def _scatter_kernel(dest_ref, x_ref, zero_ref, out_ref, sem):
    del zero_ref
    ts = x_ref.shape[0]

    def start(j, carry):
        for k in range(2):
            _row_copy(x_ref, j, out_ref, dest_ref[0, 0, 2 * j + k], sem).start(priority=k)
        return carry

    lax.fori_loop(0, ts, start, 0, unroll=ROW_DMA_UNROLL)
    for _ in range(2):
        _rows_copy(x_ref, out_ref, ts, sem).wait()


def _moe_scatter(h1, dest, n_rows):
    n_tok = h1.shape[0]
    ts = min(ROW_TILE, n_tok)
    zeros = jnp.zeros((n_rows, D_MODEL), _f32)
    return pl.pallas_call(
        _scatter_kernel,
        grid=(n_tok // ts,),
        in_specs=[pl.BlockSpec((1, 1, 2 * ts), lambda i: (i, 0, 0), memory_space=pltpu.SMEM),
                  pl.BlockSpec((ts, D_MODEL), lambda i: (i, 0)),
                  pl.BlockSpec(memory_space=pl.ANY)],
        out_specs=pl.BlockSpec(memory_space=pl.ANY),
        out_shape=jax.ShapeDtypeStruct((n_rows, D_MODEL), _f32),
        scratch_shapes=[pltpu.SemaphoreType.DMA(())],
        input_output_aliases={2: 0},
        compiler_params=_params(1),
        name="moe_scatter",
    )(dest.reshape(n_tok // ts, 1, 2 * ts), h1, zeros)


def _ffn_kernel(be_ref, nu_ref, x_ref, wg_ref, wu_ref, wd_ref, y_ref):
    del be_ref
    i = pl.program_id(0)

    @pl.when(i < nu_ref[0])
    def _():
        x = x_ref[...].astype(_bf16)
        gate = jnp.dot(x, wg_ref[...], preferred_element_type=_f32)
        up = jnp.dot(x, wu_ref[...], preferred_element_type=_f32)
        act = gate * (1.0 / (1.0 + jnp.exp(-gate))) * up
        y_ref[...] = jnp.dot(act.astype(_bf16), wd_ref[...], preferred_element_type=_f32)

    @pl.when(i >= nu_ref[0])
    def _():
        y_ref[...] = jnp.zeros(y_ref.shape, _f32)


def _moe_ffn(rows, blk_expert, n_used, wg, wu, wd):
    n_rows = rows.shape[0]
    tb = ROW_TILE
    grid_spec = pltpu.PrefetchScalarGridSpec(
        num_scalar_prefetch=2,
        grid=(n_rows // tb,),
        in_specs=[pl.BlockSpec((tb, D_MODEL), lambda i, be, nu: (i, 0)),
                  pl.BlockSpec((None, D_MODEL, D_EXPERT), lambda i, be, nu: (be[i], 0, 0)),
                  pl.BlockSpec((None, D_MODEL, D_EXPERT), lambda i, be, nu: (be[i], 0, 0)),
                  pl.BlockSpec((None, D_EXPERT, D_MODEL), lambda i, be, nu: (be[i], 0, 0))],
        out_specs=pl.BlockSpec((tb, D_MODEL), lambda i, be, nu: (i, 0)),
    )
    return pl.pallas_call(
        _ffn_kernel,
        grid_spec=grid_spec,
        out_shape=jax.ShapeDtypeStruct((n_rows, D_MODEL), _f32),
        compiler_params=_params(1),
        name="moe_ffn",
    )(blk_expert, n_used, rows, wg, wu, wd)


def _combine_kernel(dest_ref, h_ref, info_ref, y_ref, g_ref, b_ref, o_ref, buf_ref, sem, *, alpha):
    ts = h_ref.shape[0]

    def start(j, carry):
        for k in range(2):
            _row_copy(y_ref, dest_ref[0, 0, 2 * j + k], buf_ref.at[k], j, sem).start(priority=k)
        return carry

    lax.fori_loop(0, ts, start, 0, unroll=ROW_DMA_UNROLL)
    for k in range(2):
        _rows_copy(y_ref, buf_ref.at[k], ts, sem).wait()

    ffn = buf_ref[0] * info_ref[:, 2:3] + buf_ref[1] * info_ref[:, 3:4]
    o_ref[...] = _layer_norm(alpha * h_ref[...] + ffn, g_ref[...], b_ref[...])


def _moe_combine(h1, info, y, dest, ln_g, ln_b, alpha):
    n_tok = h1.shape[0]
    ts = min(ROW_TILE, n_tok)
    row = lambda i: (i, 0)
    fixed = lambda i: (0, 0)
    return pl.pallas_call(
        functools.partial(_combine_kernel, alpha=alpha),
        grid=(n_tok // ts,),
        in_specs=[pl.BlockSpec((1, 1, 2 * ts), lambda i: (i, 0, 0), memory_space=pltpu.SMEM),
                  pl.BlockSpec((ts, D_MODEL), row),
                  pl.BlockSpec((ts, LANES), row),
                  pl.BlockSpec(memory_space=pl.ANY),
                  pl.BlockSpec((1, D_MODEL), fixed),
                  pl.BlockSpec((1, D_MODEL), fixed)],
        out_specs=pl.BlockSpec((ts, D_MODEL), row),
        out_shape=jax.ShapeDtypeStruct((n_tok, D_MODEL), _f32),
        scratch_shapes=[pltpu.VMEM((2, ts, D_MODEL), _f32), pltpu.SemaphoreType.DMA(())],
        compiler_params=_params(1),
        name="moe_combine_ln",
    )(dest.reshape(n_tok // ts, 1, 2 * ts), h1, info, y, ln_g, ln_b)


def _hier_moe(h1, info, counts, wg, wu, wd, ln_g, ln_b, alpha):
    n_tok = h1.shape[0]
    tb = ROW_TILE
    cnt = counts[0, ROUTER_COL0:ROUTER_COL0 + N_EXPERTS].astype(jnp.int32)
    padded = (cnt + tb - 1) // tb * tb
    pad_end = jnp.cumsum(padded)
    pad_start = pad_end - padded
    n_rows = (2 * n_tok + N_EXPERTS * (tb - 1) + tb - 1) // tb * tb
    expert = info[:, 0:2].astype(jnp.int32)
    rank = info[:, 4:6].astype(jnp.int32)
    dest = pad_start[expert] + rank
    blk_start = jnp.arange(n_rows // tb, dtype=jnp.int32) * tb
    blk_expert = jnp.minimum(jnp.sum(pad_end[None, :] <= blk_start[:, None], axis=1),
                             N_EXPERTS - 1).astype(jnp.int32)
    n_used = (pad_end[-1:] // tb).astype(jnp.int32)
    rows = _moe_scatter(h1, dest, n_rows)
    y = _moe_ffn(rows, blk_expert, n_used, wg, wu, wd)
    return _moe_combine(h1, info, y, dest, ln_g, ln_b, alpha)


def _ple_kernel(h_ref, p_ref, wp_ref, wg_ref, g_ref, b_ref, o_ref, *, alpha):
    h = h_ref[...]
    emb = jnp.dot(p_ref[...].astype(_bf16), wp_ref[...], preferred_element_type=_f32)
    gate = jnp.dot(h.astype(_bf16), wg_ref[...], preferred_element_type=_f32)
    ple = emb * (1.0 / (1.0 + jnp.exp(-gate)))
    o_ref[...] = _layer_norm(alpha * h + ple, g_ref[...], b_ref[...])


def _ple(h, p, wp, wg, ln_g, ln_b, alpha):
    n_tok = h.shape[0]
    tm = min(ROW_TILE, n_tok)
    row = lambda i: (i, 0)
    fixed = lambda i: (0, 0)
    return pl.pallas_call(
        functools.partial(_ple_kernel, alpha=alpha),
        grid=(n_tok // tm,),
        in_specs=[pl.BlockSpec((tm, D_MODEL), row),
                  pl.BlockSpec((tm, p.shape[1]), row),
                  pl.BlockSpec(wp.shape, fixed),
                  pl.BlockSpec(wg.shape, fixed),
                  pl.BlockSpec((1, D_MODEL), fixed),
                  pl.BlockSpec((1, D_MODEL), fixed)],
        out_specs=pl.BlockSpec((tm, D_MODEL), row),
        out_shape=jax.ShapeDtypeStruct((n_tok, D_MODEL), _f32),
        compiler_params=_params(1),
        name="ple_ln",
    )(h, p, wp, wg, ln_g, ln_b)


def kernel(x, p, w_in_a, w_o_a, w_kv_shared, w_q_b, w_o_b, w_router_group, b_router_group,
           w_router_expert, b_router_expert, w_gate, w_up, w_down, w_ple, w_ple_gate, ln_g, ln_b):
    bsz, seq, _ = x.shape
    depth = w_gate.shape[0]
    n_a = w_in_a.shape[0]
    alpha = (2.0 * depth) ** 0.25
    n_tok = bsz * seq
    assert seq % DSA_K_TILE == 0 and seq % MOBA_BLOCK == 0 and n_tok % ROW_TILE == 0
    scale = LOG2E * HEAD_DIM ** -0.5
    c_k = D_MODEL + KV_DIM
    c_v = D_MODEL + 2 * KV_DIM
    c_iq = c_v + IDX_HEADS * IDX_DIM

    h = x.reshape(n_tok, D_MODEL)
    k4 = vt = km = None
    for i in range(depth):
        if i < n_a:
            w = w_in_a[i]
            w_ikw = jnp.pad(w[:, c_iq:], ((0, 0), (0, LANES - (w.shape[1] - c_iq))))
            q, k, v, iq, ikw = _project(
                h,
                [w[:, :D_MODEL].astype(_bf16), w[:, D_MODEL:c_k].astype(_bf16),
                 w[:, c_k:c_v].astype(_bf16), w[:, c_v:c_iq].astype(_bf16), w_ikw.astype(_bf16)],
                [_bf16, _bf16, _bf16, _f32, _f32],
                [scale, 1.0, 1.0, 1.0, 1.0])
            attn = _dsa_attention(q, iq, ikw, k, v, bsz, seq)
            w_o = w_o_a[i]
        else:
            j = i - n_a
            if k4 is None:
                n_blocks = seq // MOBA_BLOCK
                nbp = -(-n_blocks // 16) * 16
                k, v, km = _shared_kv(h, w_kv_shared[:, :KV_DIM].astype(_bf16),
                                      w_kv_shared[:, KV_DIM:].astype(_bf16))
                k4 = k.reshape(bsz, n_blocks, MOBA_BLOCK, KV_DIM)
                vt = _values_to_lanes(v, (bsz, n_blocks, MOBA_BLOCK))
                km = jnp.pad(km.reshape(bsz, n_blocks, KV_DIM), ((0, 0), (0, nbp - n_blocks), (0, 0)))
                km_hi = km.astype(_bf16)
                km_lo = (km - km_hi.astype(_f32)).astype(_bf16)
                km = jnp.stack([km_hi, km_lo], axis=1)
            (q,) = _project(h, [w_q_b[j].astype(_bf16)], [_bf16], [scale])
            attn = _moba_attention(q, k4, vt, km, bsz, seq)
            w_o = w_o_b[j]

        w_r = jnp.concatenate([w_router_group[i], w_router_expert[i]], axis=1)
        w_r = jnp.pad(w_r, ((0, 0), (0, LANES - w_r.shape[1])))
        w_r_hi = w_r.astype(_bf16)
        w_r = jnp.stack([w_r_hi, (w_r - w_r_hi.astype(_f32)).astype(_bf16)])
        b_r = jnp.concatenate([b_router_group[i], b_router_expert[i]])
        b_r = jnp.pad(b_r, (0, LANES - b_r.shape[0])).reshape(1, LANES)
        h1, info, counts = _mix_router(attn, w_o.astype(_bf16), h, ln_g[i, 0].reshape(1, -1),
                                       ln_b[i, 0].reshape(1, -1), w_r, b_r, alpha)
        h2 = _hier_moe(h1, info, counts, w_gate[i].astype(_bf16), w_up[i].astype(_bf16),
                       w_down[i].astype(_bf16), ln_g[i, 1].reshape(1, -1), ln_b[i, 1].reshape(1, -1),
                       alpha)
        h = _ple(h2, p[i].reshape(n_tok, -1), w_ple[i].astype(_bf16), w_ple_gate[i].astype(_bf16),
                 ln_g[i, 2].reshape(1, -1), ln_b[i, 2].reshape(1, -1), alpha)
    return h.reshape(bsz, seq, D_MODEL)
```

```python
import functools

import jax
import jax.numpy as jnp
from jax import lax
from jax.experimental import pallas as pl
from jax.experimental.pallas import tpu as pltpu

D_MODEL = 1024
N_HEADS = 8
HEAD_DIM = 128
N_KV_HEADS = 2
HEAD_GROUP = N_HEADS // N_KV_HEADS
KV_DIM = N_KV_HEADS * HEAD_DIM
IDX_HEADS = 4
IDX_DIM = 64
IDX_TOPK_MAX = 256
MOBA_BLOCK = 256
MOBA_TOPK = 3
N_GROUPS = 4
EXPERTS_PER_GROUP = 8
N_EXPERTS = N_GROUPS * EXPERTS_PER_GROUP
D_EXPERT = 512
LN_EPS = 1e-5

LANES = 128
DSA_Q_TILE = 128
DSA_K_TILE = 512
DSA_CHAIN_HEADS = 2
ROW_TILE = 256
PROJ_TILE = 512
ROW_DMA_UNROLL = 8
VMEM_LIMIT_BYTES = 52 * 1024 * 1024

ROUTER_COL0 = N_GROUPS
INT_MIN = -2 ** 31
NEG_INF_KEY = -2139095041
M_INIT = -1e30
LOG2E = 1.4426950408889634
V_ROWS = 144
COUNT_ROWS = 64
PLANE_ROWS = 256

_f32 = jnp.float32
_bf16 = jnp.bfloat16


def _params(n_axes):
    return pltpu.CompilerParams(dimension_semantics=("arbitrary",) * n_axes,
                                vmem_limit_bytes=VMEM_LIMIT_BYTES)


def _layer_norm(z, g, b):
    mu = jnp.mean(z, axis=-1, keepdims=True)
    zc = z - mu
    var = jnp.mean(zc * zc, axis=-1, keepdims=True)
    return zc * lax.rsqrt(var + LN_EPS) * g + b


def _col_max(s):
    rows = s.shape[0]
    if rows > COUNT_ROWS and rows % COUNT_ROWS == 0:
        s = jnp.max(s.reshape(rows // COUNT_ROWS, COUNT_ROWS, s.shape[1]), axis=0)
    return jnp.max(s, axis=0, keepdims=True)


def _bit_transpose32(words):
    a = list(words)
    j, m = 16, 0x0000FFFF
    while j:
        k = 0
        while k < 32:
            t = (a[k] ^ (a[k + j] >> j)) & m
            a[k] = a[k] ^ t
            a[k + j] = a[k + j] ^ (t << j)
            k = (k + j + 1) & ~j
        j >>= 1
        if j:
            m ^= m << j
    return a


def _head_slopes():
    return tuple(LOG2E * 2.0 ** (-8.0 * (h + 1) / N_HEADS) for h in range(N_HEADS))


def _proj_kernel(x_ref, *refs, scales):
    n = len(scales)
    x = x_ref[...].astype(_bf16)
    for w_ref, o_ref, s in zip(refs[:n], refs[n:], scales):
        y = jnp.dot(x, w_ref[...], preferred_element_type=_f32)
        if s != 1.0:
            y = y * s
        o_ref[...] = y.astype(o_ref.dtype)


def _project(x, weights, out_dtypes, scales):
    n_tok, k = x.shape
    tm = min(PROJ_TILE, n_tok)
    in_specs = [pl.BlockSpec((tm, k), lambda i: (i, 0))]
    in_specs += [pl.BlockSpec(w.shape, lambda i: (0, 0)) for w in weights]
    return pl.pallas_call(
        functools.partial(_proj_kernel, scales=tuple(scales)),
        grid=(n_tok // tm,),
        in_specs=in_specs,
        out_specs=[pl.BlockSpec((tm, w.shape[1]), lambda i: (i, 0)) for w in weights],
        out_shape=[jax.ShapeDtypeStruct((n_tok, w.shape[1]), dt) for w, dt in zip(weights, out_dtypes)],
        compiler_params=_params(1),
        name="proj",
    )(x, *weights)


def _kv_kernel(x_ref, wk_ref, wv_ref, k_ref, v_ref, km_ref):
    x = x_ref[...].astype(_bf16)
    k = jnp.dot(x, wk_ref[...], preferred_element_type=_f32)
    v = jnp.dot(x, wv_ref[...], preferred_element_type=_f32)
    k_ref[...] = k.astype(_bf16)
    v_ref[...] = v.astype(_bf16)
    km_ref[0] = jnp.mean(k, axis=0, keepdims=True)


def _shared_kv(x, wk, wv):
    n_tok, k = x.shape
    tm = MOBA_BLOCK
    return pl.pallas_call(
        _kv_kernel,
        grid=(n_tok // tm,),
        in_specs=[pl.BlockSpec((tm, k), lambda i: (i, 0)),
                  pl.BlockSpec(wk.shape, lambda i: (0, 0)),
                  pl.BlockSpec(wv.shape, lambda i: (0, 0))],
        out_specs=[pl.BlockSpec((tm, KV_DIM), lambda i: (i, 0)),
                   pl.BlockSpec((tm, KV_DIM), lambda i: (i, 0)),
                   pl.BlockSpec((1, 1, KV_DIM), lambda i: (i, 0, 0))],
        out_shape=[jax.ShapeDtypeStruct((n_tok, KV_DIM), _bf16),
                   jax.ShapeDtypeStruct((n_tok, KV_DIM), _bf16),
                   jax.ShapeDtypeStruct((n_tok // tm, 1, KV_DIM), _f32)],
        compiler_params=_params(1),
        name="shared_kv",
    )(x, wk, wv)


def _heads_to_lanes(a, bsz, seq, n_heads, dim):
    return a.reshape(bsz, seq, n_heads, dim).transpose(0, 2, 3, 1)


def _lanes_to_heads(a, bsz, seq):
    return a.transpose(0, 3, 1, 2).reshape(bsz * seq, -1)


def _dsa_kernel(qt_ref, iqt_ref, iwt_ref, ik_ref, k_ref, vt_ref, o_ref,
                keys_ref, planes_ref, eq_ref, qs_ref, tab_ref, m_ref, acc_ref, *, topk, seq, slopes):
    tq, tk = DSA_Q_TILE, DSA_K_TILE
    q0 = pl.program_id(1) * tq
    n_t = (q0 + tq + tk - 1) // tk
    qpos = q0 + lax.broadcasted_iota(jnp.int32, (1, tq), 1)
    kidx = lax.broadcasted_iota(jnp.int32, (tk, 1), 0)
    iw = iwt_ref[...] * (IDX_HEADS ** -0.5)

    def score_body(t, carry):
        ik = ik_ref[t]
        sc = jnp.zeros((tk, tq), _f32)
        for h in range(IDX_HEADS):
            s = jnp.dot(ik, iqt_ref[h], preferred_element_type=_f32)
            sc = sc + jnp.maximum(s * (IDX_DIM ** -0.5), 0.0) * iw[h:h + 1, :]
        kpos = t * tk + kidx
        sc = jnp.where(kpos <= qpos, sc, -jnp.inf)
        bits = pltpu.bitcast(sc, jnp.int32)
        keys = bits ^ ((bits >> 31) & 0x7FFFFFFF)
        keys = jnp.where(sc == 0.0, 0, keys)
        keys_ref[t] = keys
        words = pltpu.bitcast(keys ^ INT_MIN, jnp.uint32)
        for s in range(tk // PLANE_ROWS):
            groups = [words[s * PLANE_ROWS + 8 * i:s * PLANE_ROWS + 8 * (i + 1), :] for i in range(32)]
            row0 = pl.multiple_of((t * (tk // PLANE_ROWS) + s) * 8, 8)
            for p, plane in enumerate(_bit_transpose32(groups)):
                planes_ref[p, pl.ds(row0, 8), :] = plane
        return carry

    @pl.when((pl.program_id(0) == 0) & (pl.program_id(1) == 0))
    def _():
        planes_ref[...] = jnp.zeros(planes_ref.shape, jnp.uint32)

    lax.fori_loop(0, n_t, score_body, 0)

    def sublane_fold(m):
        part = m[0:COUNT_ROWS]
        for c in range(1, tk // COUNT_ROWS):
            part = part + m[c * COUNT_ROWS:(c + 1) * COUNT_ROWS]
        return part

    def popcount_rows(x):
        pc = pltpu.bitcast(lax.population_count(x), jnp.int32)
        r = pc.shape[0]
        if r > COUNT_ROWS:
            pc = jnp.sum(pc.reshape(r // COUNT_ROWS, COUNT_ROWS, tq), axis=0)
        pc = jnp.sum(pc.reshape(pc.shape[0] // 8, 8, tq), axis=0)
        return jnp.sum(pc.astype(_f32), axis=0, keepdims=True)

    plane_rows = planes_ref.shape[1]
    live = lax.broadcasted_iota(jnp.int32, (plane_rows, 1), 0) < n_t * (tk // PLANE_ROWS) * 8
    eq_ref[...] = jnp.broadcast_to(jnp.where(live, jnp.uint32(0xFFFFFFFF), jnp.uint32(0)),
                                   (plane_rows, tq))

    def bit_body(i, carry):
        tu, above = carry
        eq = eq_ref[...]
        ones = eq & planes_ref[i]
        total = above + popcount_rows(ones)
        hit = total >= topk
        eq_ref[...] = jnp.where(hit, ones, eq ^ ones)
        return (jnp.where(hit, tu | lax.shift_left(jnp.int32(1), 31 - i), tu),
                jnp.where(hit, above, total))

    tu, above = lax.fori_loop(0, 32, bit_body,
                              (jnp.zeros((1, tq), jnp.int32), jnp.zeros((1, tq), _f32)))
    thr = tu ^ INT_MIN
    c_ge = above + popcount_rows(eq_ref[...])
    need = topk - above

    def tie_search():
        n_bits = (seq - 1).bit_length()

        def count_eq_before(pc):
            def body(t, acc):
                kpos = t * tk + kidx
                hit = jnp.where(keys_ref[t] == thr, jnp.where(kpos < pc, 1.0, 0.0), 0.0)
                return acc + sublane_fold(hit)
            acc = lax.fori_loop(0, n_t, body, jnp.zeros((COUNT_ROWS, tq), _f32))
            return jnp.sum(acc, axis=0, keepdims=True)

        def body(i, p):
            cand = p | lax.shift_left(jnp.int32(1), n_bits - 1 - i)
            return jnp.where(count_eq_before(cand) < need, cand, p)

        return lax.fori_loop(0, n_bits, body, jnp.zeros((1, tq), jnp.int32))

    excess = jnp.where(c_ge > topk, jnp.where(thr > NEG_INF_KEY, 1.0, 0.0), 0.0)
    pcut = lax.cond(jnp.max(excess) > 0.0, tie_search,
                    lambda: jnp.full((1, tq), seq, jnp.int32))

    n_chain = N_HEADS // DSA_CHAIN_HEADS
    chain_heads = [range(c * DSA_CHAIN_HEADS, (c + 1) * DSA_CHAIN_HEADS) for c in range(n_chain)]
    for c in range(n_chain):
        qs_ref[c] = jnp.concatenate([qt_ref[h] for h in chain_heads[c]], axis=1)
    kidx_f = lax.broadcasted_iota(jnp.int32, (tk, tq), 0).astype(_f32)
    for h in range(N_HEADS):
        tab_ref[h] = slopes[h] * kidx_f
    m_ref[...] = jnp.full(m_ref.shape, M_INIT, _f32)
    acc_ref[...] = jnp.zeros(acc_ref.shape, _f32)

    def att_body(t, carry):
        keys = keys_ref[t]
        kpos = t * tk + kidx
        take = jnp.where(keys > thr, 0.0,
                         jnp.where(keys == thr, jnp.where(kpos <= pcut, 0.0, -jnp.inf), -jnp.inf))
        take = jnp.where(kpos <= qpos, take, -jnp.inf)
        base = (t * tk - qpos).astype(_f32)
        s_all, p_all, scale_all = [], [], []
        for c in range(n_chain):
            g = chain_heads[c][0] // HEAD_GROUP
            s = jnp.dot(k_ref[t, :, g * HEAD_DIM:(g + 1) * HEAD_DIM], qs_ref[c],
                        preferred_element_type=_f32)
            s_all.append(s + jnp.concatenate([take + tab_ref[h] for h in chain_heads[c]], axis=1))
        for c in range(n_chain):
            off = jnp.concatenate([slopes[h] * base for h in chain_heads[c]], axis=1)
            m_prev = m_ref[c]
            m_new = jnp.maximum(m_prev, _col_max(s_all[c]) + off)
            p_all.append(jnp.exp2(s_all[c] - (m_new - off)).astype(_bf16))
            scale_all.append(jnp.exp2(m_prev - m_new))
            m_ref[c] = m_new
        for c in range(n_chain):
            g = chain_heads[c][0] // HEAD_GROUP
            acc_ref[c] = scale_all[c] * acc_ref[c] + jnp.dot(
                vt_ref[t, g * V_ROWS:(g + 1) * V_ROWS, :], p_all[c], preferred_element_type=_f32)
        return carry

    lax.fori_loop(0, n_t, att_body, 0)

    for c in range(n_chain):
        out = acc_ref[c, 0:HEAD_DIM, :] / acc_ref[c, HEAD_DIM:HEAD_DIM + 1, :]
        for j, h in enumerate(chain_heads[c]):
            o_ref[h] = out[:, j * tq:(j + 1) * tq].astype(o_ref.dtype)


def _values_to_lanes(v, lead):
    t = lead[-1]
    vt = v.reshape(*lead[:-1], t, N_KV_HEADS, HEAD_DIM)
    vt = jnp.moveaxis(vt, -3, -1)
    ones = jnp.ones(vt.shape[:-2] + (1, t), vt.dtype)
    zeros = jnp.zeros(vt.shape[:-2] + (V_ROWS - HEAD_DIM - 1, t), vt.dtype)
    vt = jnp.concatenate([vt, ones, zeros], axis=-2)
    return vt.reshape(*lead[:-1], N_KV_HEADS * V_ROWS, t)


def _dsa_attention(q, iq, ikw, k, v, bsz, seq):
    tq, tk = DSA_Q_TILE, DSA_K_TILE
    n_t = seq // tk
    topk = min(IDX_TOPK_MAX, seq // 4)
    qt = _heads_to_lanes(q, bsz, seq, N_HEADS, HEAD_DIM)
    iqt = _heads_to_lanes(iq.astype(_bf16), bsz, seq, IDX_HEADS, IDX_DIM)
    ikw3 = ikw.reshape(bsz, seq, LANES)
    ik = ikw3[:, :, :IDX_DIM].astype(_bf16).reshape(bsz, n_t, tk, IDX_DIM)
    iwt = jnp.pad(ikw3[:, :, IDX_DIM:IDX_DIM + IDX_HEADS].transpose(0, 2, 1),
                  ((0, 0), (0, 8 - IDX_HEADS), (0, 0)))
    k4 = k.reshape(bsz, n_t, tk, KV_DIM)
    vt = _values_to_lanes(v, (bsz, n_t, tk))
    n_chain = N_HEADS // DSA_CHAIN_HEADS
    rows = DSA_CHAIN_HEADS * tq
    out = pl.pallas_call(
        functools.partial(_dsa_kernel, topk=topk, seq=seq, slopes=_head_slopes()),
        grid=(bsz, seq // tq),
        in_specs=[pl.BlockSpec((None, N_HEADS, HEAD_DIM, tq), lambda b, i: (b, 0, 0, i)),
                  pl.BlockSpec((None, IDX_HEADS, IDX_DIM, tq), lambda b, i: (b, 0, 0, i)),
                  pl.BlockSpec((None, 8, tq), lambda b, i: (b, 0, i)),
                  pl.BlockSpec((None, n_t, tk, IDX_DIM), lambda b, i: (b, 0, 0, 0)),
                  pl.BlockSpec((None, n_t, tk, KV_DIM), lambda b, i: (b, 0, 0, 0)),
                  pl.BlockSpec((None, n_t, N_KV_HEADS * V_ROWS, tk), lambda b, i: (b, 0, 0, 0))],
        out_specs=pl.BlockSpec((None, N_HEADS, HEAD_DIM, tq), lambda b, i: (b, 0, 0, i)),
        out_shape=jax.ShapeDtypeStruct((bsz, N_HEADS, HEAD_DIM, seq), _bf16),
        scratch_shapes=[pltpu.VMEM((n_t, tk, tq), jnp.int32),
                        pltpu.VMEM((32, seq // 32, tq), jnp.uint32),
                        pltpu.VMEM((seq // 32, tq), jnp.uint32),
                        pltpu.VMEM((n_chain, HEAD_DIM, rows), _bf16),
                        pltpu.VMEM((N_HEADS, tk, tq), _f32),
                        pltpu.VMEM((n_chain, 1, rows), _f32),
                        pltpu.VMEM((n_chain, V_ROWS, rows), _f32)],
        compiler_params=_params(2),
        name="dsa_attention",
    )(qt, iqt, iwt, ik, k4, vt)
    return _lanes_to_heads(out, bsz, seq)


def _moba_kernel(qt_ref, km_ref, k_ref, vt_ref, o_ref,
                 sel_ref, m_ref, acc_ref, *, n_sel, slopes):
    blk = MOBA_BLOCK
    nbp = km_ref.shape[1]
    own = pl.program_id(1)
    ownf = own.astype(_f32)
    blkid = lax.broadcasted_iota(jnp.int32, (nbp, 1), 0).astype(_f32)
    qidx = lax.broadcasted_iota(jnp.int32, (1, blk), 1)
    kidx = lax.broadcasted_iota(jnp.int32, (blk, 1), 0)
    rel = (kidx - qidx).astype(_f32)

    gates, scores, probs = [], [], []
    for h in range(N_HEADS):
        dsl = slice(h // HEAD_GROUP * HEAD_DIM, (h // HEAD_GROUP + 1) * HEAD_DIM)
        gates.append(jnp.dot(km_ref[0, :, dsl], qt_ref[h], preferred_element_type=_f32)
                     + jnp.dot(km_ref[1, :, dsl], qt_ref[h], preferred_element_type=_f32))
        scores.append(jnp.dot(k_ref[own, :, dsl], qt_ref[h], preferred_element_type=_f32))
    for h in range(N_HEADS):
        gl = jnp.where(blkid < ownf, gates[h], -jnp.inf)
        sel = jnp.zeros((nbp, blk), _f32)
        for _ in range(n_sel):
            mx = jnp.max(gl, axis=0, keepdims=True)
            cand = jnp.where(gl == mx, blkid, 1e9)
            cand = jnp.where(mx > -jnp.inf, cand, 1e9)
            pick = blkid == jnp.min(cand, axis=0, keepdims=True)
            sel = jnp.where(pick, 1.0, sel)
            gl = jnp.where(pick, -jnp.inf, gl)
        sel_ref[h] = sel
    for h in range(N_HEADS):
        x = scores[h] + jnp.where(kidx <= qidx, slopes[h] * rel, -jnp.inf)
        m = _col_max(x)
        probs.append(jnp.exp2(x - m).astype(_bf16))
        m_ref[h] = m
    for h in range(N_HEADS):
        g = h // HEAD_GROUP
        acc_ref[h] = jnp.dot(vt_ref[own, g * V_ROWS:(g + 1) * V_ROWS, :], probs[h],
                             preferred_element_type=_f32)

    def blk_body(n, carry):
        shift = ((n - own) * blk).astype(_f32)
        s_all, p_all, scale_all = [], [], []
        for h in range(N_HEADS):
            g = h // HEAD_GROUP
            s_all.append(jnp.dot(k_ref[n, :, g * HEAD_DIM:(g + 1) * HEAD_DIM], qt_ref[h],
                                 preferred_element_type=_f32) + slopes[h] * rel)
        for h in range(N_HEADS):
            rowc = jnp.where(sel_ref[h, pl.ds(n, 1), :] > 0.0, slopes[h] * shift, -jnp.inf)
            m_prev = m_ref[h]
            m_new = jnp.maximum(m_prev, _col_max(s_all[h]) + rowc)
            p_all.append(jnp.exp2(s_all[h] - (m_new - rowc)).astype(_bf16))
            scale_all.append(jnp.exp2(m_prev - m_new))
            m_ref[h] = m_new
        for h in range(N_HEADS):
            g = h // HEAD_GROUP
            acc_ref[h] = scale_all[h] * acc_ref[h] + jnp.dot(
                vt_ref[n, g * V_ROWS:(g + 1) * V_ROWS, :], p_all[h], preferred_element_type=_f32)
        return carry

    lax.fori_loop(0, own, blk_body, 0)

    for h in range(N_HEADS):
        out = acc_ref[h, 0:HEAD_DIM, :] / acc_ref[h, HEAD_DIM:HEAD_DIM + 1, :]
        o_ref[h] = out.astype(o_ref.dtype)


def _moba_attention(q, k4, vt, km, bsz, seq):
    blk = MOBA_BLOCK
    n_blocks = seq // blk
    nbp = km.shape[2]
    n_sel = max(1, min(MOBA_TOPK, n_blocks - 1))
    out = pl.pallas_call(
        functools.partial(_moba_kernel, n_sel=n_sel, slopes=_head_slopes()),
        grid=(bsz, n_blocks),
        in_specs=[pl.BlockSpec((None, N_HEADS, HEAD_DIM, blk), lambda b, i: (b, 0, 0, i)),
                  pl.BlockSpec((None, 2, nbp, KV_DIM), lambda b, i: (b, 0, 0, 0)),
                  pl.BlockSpec((None, n_blocks, blk, KV_DIM), lambda b, i: (b, 0, 0, 0)),
                  pl.BlockSpec((None, n_blocks, N_KV_HEADS * V_ROWS, blk), lambda b, i: (b, 0, 0, 0))],
        out_specs=pl.BlockSpec((None, N_HEADS, HEAD_DIM, blk), lambda b, i: (b, 0, 0, i)),
        out_shape=jax.ShapeDtypeStruct((bsz, N_HEADS, HEAD_DIM, seq), _bf16),
        scratch_shapes=[pltpu.VMEM((N_HEADS, nbp, blk), _f32),
                        pltpu.VMEM((N_HEADS, 1, blk), _f32),
                        pltpu.VMEM((N_HEADS, V_ROWS, blk), _f32)],
        compiler_params=_params(2),
        name="moba_attention",
    )(_heads_to_lanes(q, bsz, seq, N_HEADS, HEAD_DIM), km, k4, vt)
    return _lanes_to_heads(out, bsz, seq)


def _mix_router_kernel(a_ref, wo_ref, h_ref, g_ref, b_ref, wr_ref, br_ref,
                       h1_ref, info_ref, cnt_ref, carry_ref, *, alpha):
    tm = a_ref.shape[0]

    @pl.when(pl.program_id(0) == 0)
    def _():
        carry_ref[...] = jnp.zeros(carry_ref.shape, _f32)

    mix = jnp.dot(a_ref[...], wo_ref[...], preferred_element_type=_f32)
    h1 = _layer_norm(alpha * h_ref[...] + mix, g_ref[...], b_ref[...])
    h1_ref[...] = h1

    h1_hi = h1.astype(_bf16)
    h1_lo = (h1 - h1_hi.astype(_f32)).astype(_bf16)
    logits = (jnp.dot(h1_hi, wr_ref[0], preferred_element_type=_f32)
              + jnp.dot(h1_lo, wr_ref[0], preferred_element_type=_f32)
              + jnp.dot(h1_hi, wr_ref[1], preferred_element_type=_f32)) + br_ref[...]
    col = lax.broadcasted_iota(jnp.int32, (1, LANES), 1).astype(_f32)
    big = 1e9
    is_group = col < float(N_GROUPS)
    gl = jnp.where(is_group, logits, -jnp.inf)
    gmax = jnp.max(gl, axis=1, keepdims=True)
    gsel = jnp.min(jnp.where(gl == gmax, col, big), axis=1, keepdims=True)
    denom = jnp.sum(jnp.where(is_group, jnp.exp(logits - gmax), 0.0), axis=1, keepdims=True)
    g_w = 1.0 / denom

    lo = ROUTER_COL0 + EXPERTS_PER_GROUP * gsel
    in_group = jnp.where(col >= lo, jnp.where(col < lo + EXPERTS_PER_GROUP, 1.0, 0.0), 0.0)
    el = jnp.where(in_group > 0.0, logits, -jnp.inf)
    v1 = jnp.max(el, axis=1, keepdims=True)
    i1 = jnp.min(jnp.where(el == v1, col, big), axis=1, keepdims=True)
    el2 = jnp.where(col == i1, -jnp.inf, el)
    v2 = jnp.max(el2, axis=1, keepdims=True)
    i2 = jnp.min(jnp.where(el2 == v2, col, big), axis=1, keepdims=True)
    t = jnp.exp(v2 - v1)
    w1 = g_w * (1.0 / (1.0 + t))
    w2 = g_w * (t / (1.0 + t))

    o1 = jnp.where(col == i1, 1.0, 0.0)
    o2 = jnp.where(col == i2, 1.0, 0.0)
    both = o1 + o2
    ri = lax.broadcasted_iota(jnp.int32, (tm, tm), 0)
    ci = lax.broadcasted_iota(jnp.int32, (tm, tm), 1)
    earlier = jnp.where(ci < ri, 1.0, 0.0).astype(_bf16)
    before = jnp.dot(earlier, both.astype(_bf16), preferred_element_type=_f32) + carry_ref[...]
    r1 = jnp.sum(o1 * before, axis=1, keepdims=True)
    r2 = jnp.sum(o2 * before, axis=1, keepdims=True)
    carry_ref[...] = carry_ref[...] + jnp.sum(both, axis=0, keepdims=True)
    cnt_ref[...] = carry_ref[...]

    info = jnp.where(col == 0.0, i1 - ROUTER_COL0, 0.0)
    info = jnp.where(col == 1.0, i2 - ROUTER_COL0, info)
    info = jnp.where(col == 2.0, w1, info)
    info = jnp.where(col == 3.0, w2, info)
    info = jnp.where(col == 4.0, r1, info)
    info = jnp.where(col == 5.0, r2, info)
    info_ref[...] = info


def _mix_router(a, wo, h, ln_g, ln_b, wr, br, alpha):
    n_tok = h.shape[0]
    tm = min(ROW_TILE, n_tok)
    row = lambda i: (i, 0)
    fixed = lambda i: (0, 0)
    return pl.pallas_call(
        functools.partial(_mix_router_kernel, alpha=alpha),
        grid=(n_tok // tm,),
        in_specs=[pl.BlockSpec((tm, D_MODEL), row),
                  pl.BlockSpec((D_MODEL, D_MODEL), fixed),
                  pl.BlockSpec((tm, D_MODEL), row),
                  pl.BlockSpec((1, D_MODEL), fixed),
                  pl.BlockSpec((1, D_MODEL), fixed),
                  pl.BlockSpec((2, D_MODEL, LANES), lambda i: (0, 0, 0)),
                  pl.BlockSpec((1, LANES), fixed)],
        out_specs=[pl.BlockSpec((tm, D_MODEL), row),
                   pl.BlockSpec((tm, LANES), row),
                   pl.BlockSpec((1, LANES), fixed)],
        out_shape=[jax.ShapeDtypeStruct((n_tok, D_MODEL), _f32),
                   jax.ShapeDtypeStruct((n_tok, LANES), _f32),
                   jax.ShapeDtypeStruct((1, LANES), _f32)],
        scratch_shapes=[pltpu.VMEM((1, LANES), _f32)],
        compiler_params=_params(1),
        name="mix_ln_router",
    )(a, wo, h, ln_g, ln_b, wr, br)


def _row_copy(src_ref, src_row, dst_ref, dst_row, sem):
    return pltpu.make_async_copy(src_ref.at[pl.ds(src_row, 1)], dst_ref.at[pl.ds(dst_row, 1)], sem)


def _rows_copy(src_ref, dst_ref, n_rows, sem):
    return pltpu.make_async_copy(src_ref.at[pl.ds(0, n_rows)], dst_ref.at[pl.ds(0, n_rows)], sem)


def _scatter_kernel(dest_ref, x_ref, zero_ref, out_ref, sem):
    del zero_ref
    ts = x_ref.shape[0]

    def start(j, carry):
        for k in range(2):
            _row_copy(x_ref, j, out_ref, dest_ref[0, 0, 2 * j + k], sem).start(priority=k)
        return carry

    lax.fori_loop(0, ts, start, 0, unroll=ROW_DMA_UNROLL)
    for _ in range(2):
        _rows_copy(x_ref, out_ref, ts, sem).wait()


def _moe_scatter(h1, dest, n_rows):
    n_tok = h1.shape[0]
    ts = min(ROW_TILE, n_tok)
    zeros = jnp.zeros((n_rows, D_MODEL), _f32)
    return pl.pallas_call(
        _scatter_kernel,
        grid=(n_tok // ts,),
        in_specs=[pl.BlockSpec((1, 1, 2 * ts), lambda i: (i, 0, 0), memory_space=pltpu.SMEM),
                  pl.BlockSpec((ts, D_MODEL), lambda i: (i, 0)),
                  pl.BlockSpec(memory_space=pl.ANY)],
        out_specs=pl.BlockSpec(memory_space=pl.ANY),
        out_shape=jax.ShapeDtypeStruct((n_rows, D_MODEL), _f32),
        scratch_shapes=[pltpu.SemaphoreType.DMA(())],
        input_output_aliases={2: 0},
        compiler_params=_params(1),
        name="moe_scatter",
    )(dest.reshape(n_tok // ts, 1, 2 * ts), h1, zeros)


def _ffn_kernel(be_ref, nu_ref, x_ref, wg_ref, wu_ref, wd_ref, y_ref):
    del be_ref
    i = pl.program_id(0)

    @pl.when(i < nu_ref[0])
    def _():
        x = x_ref[...].astype(_bf16)
        gate = jnp.dot(x, wg_ref[...], preferred_element_type=_f32)
        up = jnp.dot(x, wu_ref[...], preferred_element_type=_f32)
        act = gate * (1.0 / (1.0 + jnp.exp(-gate))) * up
        y_ref[...] = jnp.dot(act.astype(_bf16), wd_ref[...], preferred_element_type=_f32)

    @pl.when(i >= nu_ref[0])
    def _():
        y_ref[...] = jnp.zeros(y_ref.shape, _f32)


def _moe_ffn(rows, blk_expert, n_used, wg, wu, wd):
    n_rows = rows.shape[0]
    tb = ROW_TILE
    grid_spec = pltpu.PrefetchScalarGridSpec(
        num_scalar_prefetch=2,
        grid=(n_rows // tb,),
        in_specs=[pl.BlockSpec((tb, D_MODEL), lambda i, be, nu: (i, 0)),
                  pl.BlockSpec((None, D_MODEL, D_EXPERT), lambda i, be, nu: (be[i], 0, 0)),
                  pl.BlockSpec((None, D_MODEL, D_EXPERT), lambda i, be, nu: (be[i], 0, 0)),
                  pl.BlockSpec((None, D_EXPERT, D_MODEL), lambda i, be, nu: (be[i], 0, 0))],
        out_specs=pl.BlockSpec((tb, D_MODEL), lambda i, be, nu: (i, 0)),
    )
    return pl.pallas_call(
        _ffn_kernel,
        grid_spec=grid_spec,
        out_shape=jax.ShapeDtypeStruct((n_rows, D_MODEL), _f32),
        compiler_params=_params(1),
        name="moe_ffn",
    )(blk_expert, n_used, rows, wg, wu, wd)


def _combine_kernel(dest_ref, h_ref, info_ref, y_ref, g_ref, b_ref, o_ref, buf_ref, sem, *, alpha):
    ts = h_ref.shape[0]

    def start(j, carry):
        for k in range(2):
            _row_copy(y_ref, dest_ref[0, 0, 2 * j + k], buf_ref.at[k], j, sem).start(priority=k)
        return carry

    lax.fori_loop(0, ts, start, 0, unroll=ROW_DMA_UNROLL)
    for k in range(2):
        _rows_copy(y_ref, buf_ref.at[k], ts, sem).wait()

    ffn = buf_ref[0] * info_ref[:, 2:3] + buf_ref[1] * info_ref[:, 3:4]
    o_ref[...] = _layer_norm(alpha * h_ref[...] + ffn, g_ref[...], b_ref[...])


def _moe_combine(h1, info, y, dest, ln_g, ln_b, alpha):
    n_tok = h1.shape[0]
    ts = min(ROW_TILE, n_tok)
    row = lambda i: (i, 0)
    fixed = lambda i: (0, 0)
    return pl.pallas_call(
        functools.partial(_combine_kernel, alpha=alpha),
        grid=(n_tok // ts,),
        in_specs=[pl.BlockSpec((1, 1, 2 * ts), lambda i: (i, 0, 0), memory_space=pltpu.SMEM),
                  pl.BlockSpec((ts, D_MODEL), row),
                  pl.BlockSpec((ts, LANES), row),
                  pl.BlockSpec(memory_space=pl.ANY),
                  pl.BlockSpec((1, D_MODEL), fixed),
                  pl.BlockSpec((1, D_MODEL), fixed)],
        out_specs=pl.BlockSpec((ts, D_MODEL), row),
        out_shape=jax.ShapeDtypeStruct((n_tok, D_MODEL), _f32),
        scratch_shapes=[pltpu.VMEM((2, ts, D_MODEL), _f32), pltpu.SemaphoreType.DMA(())],
        compiler_params=_params(1),
        name="moe_combine_ln",
    )(dest.reshape(n_tok // ts, 1, 2 * ts), h1, info, y, ln_g, ln_b)


def _hier_moe(h1, info, counts, wg, wu, wd, ln_g, ln_b, alpha):
    n_tok = h1.shape[0]
    tb = ROW_TILE
    cnt = counts[0, ROUTER_COL0:ROUTER_COL0 + N_EXPERTS].astype(jnp.int32)
    padded = (cnt + tb - 1) // tb * tb
    pad_end = jnp.cumsum(padded)
    pad_start = pad_end - padded
    n_rows = (2 * n_tok + N_EXPERTS * (tb - 1) + tb - 1) // tb * tb
    expert = info[:, 0:2].astype(jnp.int32)
    rank = info[:, 4:6].astype(jnp.int32)
    dest = pad_start[expert] + rank
    blk_start = jnp.arange(n_rows // tb, dtype=jnp.int32) * tb
    blk_expert = jnp.minimum(jnp.sum(pad_end[None, :] <= blk_start[:, None], axis=1),
                             N_EXPERTS - 1).astype(jnp.int32)
    n_used = (pad_end[-1:] // tb).astype(jnp.int32)
    rows = _moe_scatter(h1, dest, n_rows)
    y = _moe_ffn(rows, blk_expert, n_used, wg, wu, wd)
    return _moe_combine(h1, info, y, dest, ln_g, ln_b, alpha)


def _ple_kernel(h_ref, p_ref, wp_ref, wg_ref, g_ref, b_ref, o_ref, *, alpha):
    h = h_ref[...]
    emb = jnp.dot(p_ref[...].astype(_bf16), wp_ref[...], preferred_element_type=_f32)
    gate = jnp.dot(h.astype(_bf16), wg_ref[...], preferred_element_type=_f32)
    ple = emb * (1.0 / (1.0 + jnp.exp(-gate)))
    o_ref[...] = _layer_norm(alpha * h + ple, g_ref[...], b_ref[...])


def _ple(h, p, wp, wg, ln_g, ln_b, alpha):
    n_tok = h.shape[0]
    tm = min(ROW_TILE, n_tok)
    row = lambda i: (i, 0)
    fixed = lambda i: (0, 0)
    return pl.pallas_call(
        functools.partial(_ple_kernel, alpha=alpha),
        grid=(n_tok // tm,),
        in_specs=[pl.BlockSpec((tm, D_MODEL), row),
                  pl.BlockSpec((tm, p.shape[1]), row),
                  pl.BlockSpec(wp.shape, fixed),
                  pl.BlockSpec(wg.shape, fixed),
                  pl.BlockSpec((1, D_MODEL), fixed),
                  pl.BlockSpec((1, D_MODEL), fixed)],
        out_specs=pl.BlockSpec((tm, D_MODEL), row),
        out_shape=jax.ShapeDtypeStruct((n_tok, D_MODEL), _f32),
        compiler_params=_params(1),
        name="ple_ln",
    )(h, p, wp, wg, ln_g, ln_b)


def kernel(x, p, w_in_a, w_o_a, w_kv_shared, w_q_b, w_o_b, w_router_group, b_router_group,
           w_router_expert, b_router_expert, w_gate, w_up, w_down, w_ple, w_ple_gate, ln_g, ln_b):
    bsz, seq, _ = x.shape
    depth = w_gate.shape[0]
    n_a = w_in_a.shape[0]
    alpha = (2.0 * depth) ** 0.25
    n_tok = bsz * seq
    assert seq % DSA_K_TILE == 0 and seq % MOBA_BLOCK == 0 and n_tok % ROW_TILE == 0
    scale = LOG2E * HEAD_DIM ** -0.5
    c_k = D_MODEL + KV_DIM
    c_v = D_MODEL + 2 * KV_DIM
    c_iq = c_v + IDX_HEADS * IDX_DIM

    h = x.reshape(n_tok, D_MODEL)
    k4 = vt = km = None
    for i in range(depth):
        if i < n_a:
            w = w_in_a[i]
            w_ikw = jnp.pad(w[:, c_iq:], ((0, 0), (0, LANES - (w.shape[1] - c_iq))))
            q, k, v, iq, ikw = _project(
                h,
                [w[:, :D_MODEL].astype(_bf16), w[:, D_MODEL:c_k].astype(_bf16),
                 w[:, c_k:c_v].astype(_bf16), w[:, c_v:c_iq].astype(_bf16), w_ikw.astype(_bf16)],
                [_bf16, _bf16, _bf16, _f32, _f32],
                [scale, 1.0, 1.0, 1.0, 1.0])
            attn = _dsa_attention(q, iq, ikw, k, v, bsz, seq)
            w_o = w_o_a[i]
        else:
            j = i - n_a
            if k4 is None:
                n_blocks = seq // MOBA_BLOCK
                nbp = -(-n_blocks // 16) * 16
                k, v, km = _shared_kv(h, w_kv_shared[:, :KV_DIM].astype(_bf16),
                                      w_kv_shared[:, KV_DIM:].astype(_bf16))
                k4 = k.reshape(bsz, n_blocks, MOBA_BLOCK, KV_DIM)
                vt = _values_to_lanes(v, (bsz, n_blocks, MOBA_BLOCK))
                km = jnp.pad(km.reshape(bsz, n_blocks, KV_DIM), ((0, 0), (0, nbp - n_blocks), (0, 0)))
                km_hi = km.astype(_bf16)
                km_lo = (km - km_hi.astype(_f32)).astype(_bf16)
                km = jnp.stack([km_hi, km_lo], axis=1)
            (q,) = _project(h, [w_q_b[j].astype(_bf16)], [_bf16], [scale])
            attn = _moba_attention(q, k4, vt, km, bsz, seq)
            w_o = w_o_b[j]

        w_r = jnp.concatenate([w_router_group[i], w_router_expert[i]], axis=1)
        w_r = jnp.pad(w_r, ((0, 0), (0, LANES - w_r.shape[1])))
        w_r_hi = w_r.astype(_bf16)
        w_r = jnp.stack([w_r_hi, (w_r - w_r_hi.astype(_f32)).astype(_bf16)])
        b_r = jnp.concatenate([b_router_group[i], b_router_expert[i]])
        b_r = jnp.pad(b_r, (0, LANES - b_r.shape[0])).reshape(1, LANES)
        h1, info, counts = _mix_router(attn, w_o.astype(_bf16), h, ln_g[i, 0].reshape(1, -1),
                                       ln_b[i, 0].reshape(1, -1), w_r, b_r, alpha)
        h2 = _hier_moe(h1, info, counts, w_gate[i].astype(_bf16), w_up[i].astype(_bf16),
                       w_down[i].astype(_bf16), ln_g[i, 1].reshape(1, -1), ln_b[i, 1].reshape(1, -1),
                       alpha)
        h = _ple(h2, p[i].reshape(n_tok, -1), w_ple[i].astype(_bf16), w_ple_gate[i].astype(_bf16),
                 ln_g[i, 2].reshape(1, -1), ln_b[i, 2].reshape(1, -1), alpha)
    return h.reshape(bsz, seq, D_MODEL)
```
